```python
import math
import jax, jax.numpy as jnp
from jax import lax
import numpy as np

D_MODEL = 1024
BATCH = 8
SEQ = 2048
DEPTH = 2

ATT_HEAD_DIM = 64
ATT_HEADS = D_MODEL // ATT_HEAD_DIM
ATT_WIDTH = ATT_HEADS * ATT_HEAD_DIM
ROT_DIM = ATT_HEAD_DIM // 4
ROPE_THETA = 500000.0
MOBA_BLOCK = 256
MOBA_TOPK = 3
MOBA_Q_CHUNK = 16

SSM_EXPAND = 2
D_INNER = SSM_EXPAND * D_MODEL
SSM_HEAD_DIM = 64
SSM_HEADS = D_INNER // SSM_HEAD_DIM
SSM_GROUPS = 8
SSM_STATE = 128
SSM_CONV = 4
SSM_CHUNK = 256
XBC_DIM = D_INNER + 2 * SSM_GROUPS * SSM_STATE

D_FF = 4 * D_MODEL

ALPHA = (2 * DEPTH) ** 0.25
BETA = (8 * DEPTH) ** -0.25
LN_EPS = 1e-5
RMS_EPS = 1e-5

IN_SPLITS = (ATT_WIDTH, ATT_WIDTH, ATT_WIDTH, D_INNER, XBC_DIM, SSM_HEADS, D_MODEL, D_MODEL)
IN_WIDTH = sum(IN_SPLITS)

kernel_name = 'moba_ssd_gated_hybrid_deepnorm'


def _split(a, sizes):
    offsets = [int(o) for o in np.cumsum(sizes)[:-1]]
    return jnp.split(a, offsets, axis=-1)


def _pad_seq(t, s_pad):
    widths = [(0, 0)] * t.ndim
    widths[1] = (0, s_pad - t.shape[1])
    return jnp.pad(t, widths)


def layer_norm(x, g, b):
    xf = x.astype(jnp.float32)
    mu = xf.mean(-1, keepdims=True)
    var = jnp.square(xf - mu).mean(-1, keepdims=True)
    return ((xf - mu) * lax.rsqrt(var + LN_EPS) * g.astype(jnp.float32) + b.astype(jnp.float32)).astype(x.dtype)


def partial_rope(t, positions):
    half = ROT_DIM // 2
    inv_freq = ROPE_THETA ** (-jnp.arange(0, ROT_DIM, 2, dtype=jnp.float32) / ROT_DIM)
    ang = positions.astype(jnp.float32)[..., None] * inv_freq
    cos = jnp.cos(ang)[:, :, None, :]
    sin = jnp.sin(ang)[:, :, None, :]
    tf = t.astype(jnp.float32)
    x1, x2 = tf[..., :half], tf[..., half:ROT_DIM]
    out = jnp.concatenate([x1 * cos - x2 * sin, x2 * cos + x1 * sin, tf[..., ROT_DIM:]], axis=-1)
    return out.astype(t.dtype)


def moba_attention(q, k, v):
    bsz, nh, s, hd = q.shape
    nblk = -(-s // MOBA_BLOCK)
    s_pad = nblk * MOBA_BLOCK
    pad = ((0, 0), (0, 0), (0, s_pad - s), (0, 0))
    q, k, v = jnp.pad(q, pad), jnp.pad(k, pad), jnp.pad(v, pad)
    kb = k.reshape(bsz, nh, nblk, MOBA_BLOCK, hd)
    vb = v.reshape(bsz, nh, nblk, MOBA_BLOCK, hd)
    k_mean = kb.astype(jnp.float32).mean(axis=3)
    gate = jnp.einsum('bhsd,bhnd->bhsn', q.astype(jnp.float32), k_mean)
    q_blk = jnp.arange(s_pad) // MOBA_BLOCK
    past = jnp.arange(nblk)[None, :] < q_blk[:, None]
    gate = jnp.where(past, gate, -jnp.inf)
    topk = min(MOBA_TOPK, nblk)
    _, sel = lax.top_k(gate, topk)
    nq = s_pad // MOBA_Q_CHUNK
    q_c = jnp.moveaxis(q.reshape(bsz, nh, nq, MOBA_Q_CHUNK, hd), 2, 0)
    sel_c = jnp.moveaxis(sel.reshape(bsz, nh, nq, MOBA_Q_CHUNK, topk), 2, 0)
    bi = jnp.arange(bsz)[:, None, None, None]
    hi = jnp.arange(nh)[None, :, None, None]
    scale = hd ** -0.5
    n_sel = topk * MOBA_BLOCK

    def attend(args):
        ci, qc, selc = args
        qpos = ci * MOBA_Q_CHUNK + jnp.arange(MOBA_Q_CHUNK)
        own = (ci * MOBA_Q_CHUNK) // MOBA_BLOCK
        k_sel = kb[bi, hi, selc]
        v_sel = vb[bi, hi, selc]
        k_own = lax.dynamic_index_in_dim(kb, own, axis=2, keepdims=False)
        v_own = lax.dynamic_index_in_dim(vb, own, axis=2, keepdims=False)
        s_sel = jnp.einsum('bhqd,bhqtkd->bhqtk', qc, k_sel).reshape(bsz, nh, MOBA_Q_CHUNK, n_sel)
        s_own = jnp.einsum('bhqd,bhkd->bhqk', qc, k_own)
        valid_sel = jnp.arange(topk)[None, :] < (qpos // MOBA_BLOCK)[:, None]
        valid_sel = jnp.repeat(valid_sel, MOBA_BLOCK, axis=1)
        valid_own = (own * MOBA_BLOCK + jnp.arange(MOBA_BLOCK))[None, :] <= qpos[:, None]
        valid = jnp.concatenate([valid_sel, valid_own], axis=1)
        scores = jnp.concatenate([s_sel, s_own], axis=-1).astype(jnp.float32) * scale
        probs = jax.nn.softmax(jnp.where(valid, scores, -jnp.inf), axis=-1)
        p_sel = probs[..., :n_sel].reshape(bsz, nh, MOBA_Q_CHUNK, topk, MOBA_BLOCK).astype(v.dtype)
        p_own = probs[..., n_sel:].astype(v.dtype)
        return (jnp.einsum('bhqtk,bhqtkd->bhqd', p_sel, v_sel)
                + jnp.einsum('bhqk,bhkd->bhqd', p_own, v_own))

    out = lax.map(attend, (jnp.arange(nq), q_c, sel_c))
    out = jnp.moveaxis(out, 0, 2).reshape(bsz, nh, s_pad, hd)
    return out[:, :, :s]


def causal_dwconv(u, w, bias):
    kw = w.shape[0]
    out = lax.conv_general_dilated(
        u, w[:, None, :].astype(u.dtype), window_strides=(1,), padding=[(kw - 1, 0)],
        dimension_numbers=('NWC', 'WIO', 'NWC'), feature_group_count=u.shape[-1])
    return out + bias.astype(u.dtype)


def ssd_chunked(xs, dt, a, bm, cm):
    f32 = jnp.float32
    bsz, l, h, p = xs.shape
    g, n = bm.shape[2], bm.shape[3]
    hg = h // g
    q = SSM_CHUNK
    nc = l // q
    xdt = (xs.astype(f32) * dt[..., None]).reshape(bsz, nc, q, g, hg, p)
    a_dt = (dt * a).reshape(bsz, nc, q, g, hg).transpose(0, 1, 3, 4, 2)
    a_cs = jnp.cumsum(a_dt, axis=-1)
    bc = bm.astype(f32).reshape(bsz, nc, q, g, n)
    cc = cm.astype(f32).reshape(bsz, nc, q, g, n)
    causal = jnp.tril(jnp.ones((q, q), dtype=bool))
    seg = a_cs[..., :, None] - a_cs[..., None, :]
    decay = jnp.where(causal, jnp.exp(jnp.where(causal, seg, 0.0)), 0.0)
    cb = jnp.einsum('bclgn,bcsgn->bcgls', cc, bc)
    y_diag = jnp.einsum('bcgkls,bcsgkp->bclgkp', cb[:, :, :, None] * decay, xdt)
    decay_states = jnp.exp(a_cs[..., -1:] - a_cs)
    xw = xdt * decay_states.transpose(0, 1, 4, 2, 3)[..., None]
    states = jnp.einsum('bcsgn,bcsgkp->bcgkpn', bc, xw)
    chunk_decay = jnp.exp(a_cs[..., -1])

    def step(carry, inp):
        st, dec = inp
        return carry * dec[..., None, None] + st, carry

    init = jnp.zeros((bsz, g, hg, p, n), f32)
    _, prev = lax.scan(step, init, (jnp.moveaxis(states, 1, 0), jnp.moveaxis(chunk_decay, 1, 0)))
    prev = jnp.moveaxis(prev, 0, 1)
    y_off = (jnp.einsum('bclgn,bcgkpn->bclgkp', cc, prev)
             * jnp.exp(a_cs).transpose(0, 1, 4, 2, 3)[..., None])
    return (y_diag + y_off).reshape(bsz, l, h, p)


def gated_rmsnorm(y, z, w):
    hf = y.astype(jnp.float32) * jax.nn.silu(z.astype(jnp.float32))
    hgp = hf.reshape(*hf.shape[:-1], SSM_GROUPS, D_INNER // SSM_GROUPS)
    hgp = hgp * lax.rsqrt(jnp.mean(jnp.square(hgp), axis=-1, keepdims=True) + RMS_EPS)
    return hgp.reshape(hf.shape) * w.astype(jnp.float32)


def mamba2_branch(z, xbc, dt_raw, conv_w, conv_b, dt_bias, a_log, d_skip, ssm_norm_w):
    bsz, s, _ = xbc.shape
    xbc = jax.nn.silu(causal_dwconv(xbc, conv_w, conv_b))
    xs, bm, cm = _split(xbc, (D_INNER, SSM_GROUPS * SSM_STATE, SSM_GROUPS * SSM_STATE))
    dt = jax.nn.softplus(dt_raw.astype(jnp.float32) + dt_bias.astype(jnp.float32))
    a = -jnp.exp(a_log.astype(jnp.float32))
    s_pad = -(-s // SSM_CHUNK) * SSM_CHUNK
    xh = xs.reshape(bsz, s, SSM_HEADS, SSM_HEAD_DIM)
    y = ssd_chunked(_pad_seq(xh, s_pad), _pad_seq(dt, s_pad), a,
                    _pad_seq(bm.reshape(bsz, s, SSM_GROUPS, SSM_STATE), s_pad),
                    _pad_seq(cm.reshape(bsz, s, SSM_GROUPS, SSM_STATE), s_pad))[:, :s]
    y = y + xh.astype(jnp.float32) * d_skip.astype(jnp.float32)[:, None]
    return gated_rmsnorm(y.reshape(bsz, s, D_INNER), z, ssm_norm_w).astype(xbc.dtype)


def hybrid_mixer(x, positions, w_in, conv_w, conv_b, dt_bias, a_log, d_skip, ssm_norm_w,
                 w_attn_proj, w_ssm_proj, w_out):
    bsz, s, _ = x.shape
    q, k, v, z, xbc, dt_raw, g_att, g_ssm = _split(x @ w_in, IN_SPLITS)
    heads = (bsz, s, ATT_HEADS, ATT_HEAD_DIM)
    q = partial_rope(q.reshape(heads), positions).transpose(0, 2, 1, 3)
    k = partial_rope(k.reshape(heads), positions).transpose(0, 2, 1, 3)
    v = v.reshape(heads).transpose(0, 2, 1, 3)
    att = moba_attention(q, k, v).transpose(0, 2, 1, 3).reshape(bsz, s, ATT_WIDTH)
    y_ssm = mamba2_branch(z, xbc, dt_raw, conv_w, conv_b, dt_bias, a_log, d_skip, ssm_norm_w)
    merged = (jax.nn.sigmoid(g_att) * (att @ w_attn_proj)
              + jax.nn.sigmoid(g_ssm) * (y_ssm @ w_ssm_proj))
    return merged @ w_out


def squared_relu_mlp(h, w_up, w_down):
    return jnp.square(jax.nn.relu(h @ w_up)) @ w_down


def setup_inputs(seed: int = 0) -> dict:
    key = jax.random.key(seed)
    ks = jax.random.split(key, 20)
    f32 = jnp.float32

    def nrm(k, shape, scale):
        return jax.random.normal(k, shape, f32) * scale

    x = jax.random.normal(ks[0], (BATCH, SEQ, D_MODEL), f32)
    start = jax.random.randint(ks[1], (BATCH, 1), 0, 1024)
    positions = (start + jnp.arange(SEQ)[None, :]).astype(jnp.int32)
    ln1_g = 1.0 + nrm(ks[2], (DEPTH, D_MODEL), 0.02)
    ln1_b = nrm(ks[3], (DEPTH, D_MODEL), 0.02)
    w_in = nrm(ks[4], (DEPTH, D_MODEL, IN_WIDTH), D_MODEL ** -0.5)
    conv_w = nrm(ks[5], (DEPTH, SSM_CONV, XBC_DIM), SSM_CONV ** -0.5)
    conv_b = nrm(ks[6], (DEPTH, XBC_DIM), 0.01)
    u = jax.random.uniform(ks[7], (DEPTH, SSM_HEADS), f32)
    dt0 = jnp.exp(u * (math.log(0.1) - math.log(0.001)) + math.log(0.001))
    dt_bias = dt0 + jnp.log(-jnp.expm1(-dt0))
    a_log = jnp.log(jax.random.uniform(ks[8], (DEPTH, SSM_HEADS), f32, 1.0, 16.0))
    d_skip = 1.0 + nrm(ks[9], (DEPTH, SSM_HEADS), 0.1)
    ssm_norm_w = 1.0 + nrm(ks[10], (DEPTH, D_INNER), 0.02)
    w_attn_proj = nrm(ks[11], (DEPTH, ATT_WIDTH, D_MODEL), BETA * ATT_WIDTH ** -0.5)
    w_ssm_proj = nrm(ks[12], (DEPTH, D_INNER, D_MODEL), BETA * D_INNER ** -0.5)
    w_out = nrm(ks[13], (DEPTH, D_MODEL, D_MODEL), BETA * D_MODEL ** -0.5)
    ln2_g = 1.0 + nrm(ks[14], (DEPTH, D_MODEL), 0.02)
    ln2_b = nrm(ks[15], (DEPTH, D_MODEL), 0.02)
    w_up = nrm(ks[16], (DEPTH, D_MODEL, D_FF), D_MODEL ** -0.5)
    w_down = nrm(ks[17], (DEPTH, D_FF, D_MODEL), BETA * D_FF ** -0.5)
    return {'x': x, 'positions': positions, 'ln1_g': ln1_g, 'ln1_b': ln1_b, 'w_in': w_in,
            'conv_w': conv_w, 'conv_b': conv_b, 'dt_bias': dt_bias, 'a_log': a_log,
            'd_skip': d_skip, 'ssm_norm_w': ssm_norm_w, 'w_attn_proj': w_attn_proj,
            'w_ssm_proj': w_ssm_proj, 'w_out': w_out, 'ln2_g': ln2_g, 'ln2_b': ln2_b,
            'w_up': w_up, 'w_down': w_down}


def reference(x, positions, ln1_g, ln1_b, w_in, conv_w, conv_b, dt_bias, a_log, d_skip,
              ssm_norm_w, w_attn_proj, w_ssm_proj, w_out, ln2_g, ln2_b, w_up, w_down):
    for i in range(DEPTH):
        mix = hybrid_mixer(x, positions, w_in[i], conv_w[i], conv_b[i], dt_bias[i], a_log[i],
                           d_skip[i], ssm_norm_w[i], w_attn_proj[i], w_ssm_proj[i], w_out[i])
        x = layer_norm(ALPHA * x + mix, ln1_g[i], ln1_b[i])
        x = layer_norm(ALPHA * x + squared_relu_mlp(x, w_up[i], w_down[i]), ln2_g[i], ln2_b[i])
    return x
```

```python
import functools

import jax
import jax.numpy as jnp
from jax import lax
from jax.experimental import pallas as pl
from jax.experimental.pallas import tpu as pltpu

F32 = jnp.float32
BF16 = jnp.bfloat16

D_MODEL = 1024
DEPTH = 2
ATT_HEAD_DIM = 64
ATT_HEADS = D_MODEL // ATT_HEAD_DIM
ATT_WIDTH = ATT_HEADS * ATT_HEAD_DIM
ROT_DIM = ATT_HEAD_DIM // 4
ROT_HALF = ROT_DIM // 2
ROPE_THETA = 500000.0
MOBA_BLOCK = 256
MOBA_TOPK = 3
D_INNER = 2 * D_MODEL
SSM_HEAD_DIM = 64
SSM_HEADS = D_INNER // SSM_HEAD_DIM
SSM_GROUPS = 8
SSM_STATE = 128
SSM_CONV = 4
SSM_CHUNK = 256
XBC_DIM = D_INNER + 2 * SSM_GROUPS * SSM_STATE
D_FF = 4 * D_MODEL
ALPHA = (2 * DEPTH) ** 0.25
LN_EPS = 1e-5
RMS_EPS = 1e-5
IN_SPLITS = (ATT_WIDTH, ATT_WIDTH, ATT_WIDTH, D_INNER, XBC_DIM, SSM_HEADS, D_MODEL, D_MODEL)

LANES = 128
SUBLANES = 8
VMEM_LIMIT = 56 * 1024 * 1024

XBC_OFF = 0
Z_OFF = XBC_OFF + XBC_DIM
Q_OFF = Z_OFF + D_INNER
K_OFF = Q_OFF + ATT_WIDTH
V_OFF = K_OFF + ATT_WIDTH
GA_OFF = V_OFF + ATT_WIDTH
GS_OFF = GA_OFF + D_MODEL
PROJ_WIDTH = GS_OFF + D_MODEL

HEADS_PER_GROUP = SSM_HEADS // SSM_GROUPS
GROUP_WIDTH = HEADS_PER_GROUP * SSM_HEAD_DIM
NEG_INF = float("-inf")


def _params(semantics):
    return pltpu.CompilerParams(dimension_semantics=semantics, vmem_limit_bytes=VMEM_LIMIT)


def _dot(a, b):
    return jnp.dot(a, b, preferred_element_type=F32)


def _dot_nt(a, b, precision=None):
    return lax.dot_general(a, b, (((1,), (1,)), ((), ())), precision=precision,
                           preferred_element_type=F32)


def _sigmoid(x):
    return 1.0 / (1.0 + jnp.exp(-x))


def _matmul_kernel(x_ref, w_ref, o_ref, xb_ref):
    @pl.when(pl.program_id(1) == 0)
    def _():
        xb_ref[...] = x_ref[...].astype(BF16)

    o_ref[...] = _dot(xb_ref[...], w_ref[...]).astype(o_ref.dtype)


def _matmul(x, w, out_dtype, tm, tn):
    m, k = x.shape
    n = w.shape[1]
    return pl.pallas_call(
        _matmul_kernel,
        grid=(m // tm, n // tn),
        in_specs=[pl.BlockSpec((tm, k), lambda i, j: (i, 0)),
                  pl.BlockSpec((k, tn), lambda i, j: (0, j))],
        out_specs=pl.BlockSpec((tm, tn), lambda i, j: (i, j)),
        out_shape=jax.ShapeDtypeStruct((m, n), out_dtype),
        scratch_shapes=[pltpu.VMEM((tm, k), BF16)],
        compiler_params=_params(("parallel", "arbitrary")),
        name="proj_matmul",
    )(x, w)


def _attn_kernel(pos_ref, invf_ref, q_ref, k_ref, v_ref, o_ref,
                 cos_s, sa_s, sb_s, q0_s, q1_s, k_s, sel_s):
    seq = q_ref.shape[1]
    nblk = seq // MOBA_BLOCK
    blk = MOBA_BLOCK
    lane = lax.broadcasted_iota(jnp.int32, (1, LANES), 1)
    head_of_lane = lane // ATT_HEAD_DIM
    dim_of_lane = lane % ATT_HEAD_DIM

    @pl.when(pl.program_id(1) == 0)
    def _():
        ang = pos_ref[0].astype(F32) * invf_ref[...]
        cos = jnp.cos(ang)
        sin = jnp.sin(ang)
        cos_s[...] = cos
        sa_s[...] = jnp.where(dim_of_lane < ROT_HALF, -sin, 0.0)
        sb_s[...] = jnp.where((dim_of_lane >= ROT_HALF) & (dim_of_lane < ROT_DIM), sin, 0.0)

    def rope(t, rows):
        return (t * cos_s[rows, :]
                + pltpu.roll(t, LANES - ROT_HALF, 1) * sa_s[rows, :]
                + pltpu.roll(t, ROT_HALF, 1) * sb_s[rows, :])

    kmeans = []
    for j in range(nblk):
        rows = pl.ds(j * blk, blk)
        kr = rope(k_ref[0, rows, :].astype(F32), rows)
        k_s[rows, :] = kr.astype(BF16)
        kmeans.append(jnp.sum(kr, axis=0, keepdims=True) * (1.0 / blk))
    kmean = jnp.concatenate(kmeans, axis=0)

    row_n = lax.broadcasted_iota(jnp.int32, (nblk, blk), 0)
    scale = ATT_HEAD_DIM ** -0.5
    for qb in range(nblk):
        rows = pl.ds(qb * blk, blk)
        qr = rope(q_ref[0, rows, :].astype(F32), rows)
        qs = (qr * scale).astype(BF16)
        q0_s[rows, :] = jnp.where(head_of_lane == 0, qs, jnp.zeros_like(qs))
        q1_s[rows, :] = jnp.where(head_of_lane == 1, qs, jnp.zeros_like(qs))
        sels = []
        for h in range(2):
            past = jnp.where(row_n < qb, 1.0, 0.0)
            if qb <= MOBA_TOPK:
                sel = past
            else:
                kmh = jnp.where(head_of_lane == h, kmean, 0.0)
                gate = _dot_nt(kmh, qr, precision=lax.Precision.HIGHEST)
                cnt = jnp.zeros((nblk, blk), F32)
                for m in range(qb):
                    gm = gate[m:m + 1, :]
                    ge = jnp.where(gm >= gate, 1.0, 0.0)
                    gt = jnp.where(gm > gate, 1.0, 0.0)
                    cnt = cnt + jnp.where(row_n > m, ge, jnp.where(row_n < m, gt, 0.0))
                sel = jnp.where(cnt < MOBA_TOPK, past, 0.0)
            sels.append(sel)
        pad = jnp.zeros((LANES - 2 * nblk, blk), F32)
        sel_s[rows, :] = jnp.transpose(jnp.concatenate(sels + [pad], axis=0))

    row_q = lax.broadcasted_iota(jnp.int32, (blk, blk), 0)
    col_k = lax.broadcasted_iota(jnp.int32, (blk, blk), 1)
    causal = col_k <= row_q

    def q_block(qb, carry):
        r0 = pl.multiple_of(qb * blk, blk)
        rows = pl.ds(r0, blk)
        qh = (q0_s[rows, :], q1_s[rows, :])
        selc = sel_s[rows, :]
        k_own = k_s[rows, :]
        v_own = v_ref[0, rows, :]
        state = []
        for h in range(2):
            s = jnp.where(causal, _dot_nt(qh[h], k_own), NEG_INF)
            m = jnp.max(s, axis=1, keepdims=True)
            p = jnp.exp(s - m)
            l = jnp.sum(p, axis=1, keepdims=True)
            acc = _dot(p.astype(BF16), v_own)
            state += [m, l, acc]

        def past_block(j, st):
            c0 = pl.multiple_of(j * blk, blk)
            kj = k_s[pl.ds(c0, blk), :]
            vj = v_ref[0, pl.ds(c0, blk), :]
            new = []
            for h in range(2):
                m, l, acc = st[3 * h:3 * h + 3]
                picked = jnp.sum(jnp.where(lane == h * nblk + j, selc, 0.0), axis=1, keepdims=True) > 0.5
                s = jnp.where(picked, _dot_nt(qh[h], kj), NEG_INF)
                m_new = jnp.maximum(m, jnp.max(s, axis=1, keepdims=True))
                alpha = jnp.exp(m - m_new)
                p = jnp.exp(s - m_new)
                l = alpha * l + jnp.sum(p, axis=1, keepdims=True)
                acc = alpha * acc + _dot(p.astype(BF16), vj)
                new += [m_new, l, acc]
            return tuple(new)

        state = lax.fori_loop(0, qb, past_block, tuple(state))
        out0 = state[2] / state[1]
        out1 = state[5] / state[4]
        o_ref[0, rows, :] = jnp.where(head_of_lane == 0, out0, out1).astype(o_ref.dtype)
        return carry

    lax.fori_loop(0, nblk, q_block, 0)


def _attention(proj3, pos3, invf):
    bsz, seq, _ = proj3.shape
    npairs = ATT_HEADS // 2
    qkv_spec = lambda off: pl.BlockSpec((1, seq, LANES), lambda b, p: (b, 0, off // LANES + p))
    return pl.pallas_call(
        _attn_kernel,
        grid=(bsz, npairs),
        in_specs=[pl.BlockSpec((1, seq, 1), lambda b, p: (b, 0, 0)),
                  pl.BlockSpec((1, LANES), lambda b, p: (0, 0)),
                  qkv_spec(Q_OFF), qkv_spec(K_OFF), qkv_spec(V_OFF)],
        out_specs=pl.BlockSpec((1, seq, LANES), lambda b, p: (b, 0, p)),
        out_shape=jax.ShapeDtypeStruct((bsz, seq, ATT_WIDTH), BF16),
        scratch_shapes=[pltpu.VMEM((seq, LANES), F32), pltpu.VMEM((seq, LANES), F32),
                        pltpu.VMEM((seq, LANES), F32),
                        pltpu.VMEM((seq, LANES), BF16), pltpu.VMEM((seq, LANES), BF16),
                        pltpu.VMEM((seq, LANES), BF16), pltpu.VMEM((seq, LANES), F32)],
        compiler_params=_params(("parallel", "arbitrary")),
        name="moba_attention",
    )(pos3, invf, proj3, proj3, proj3)


def _split_pack(v, lane):
    hi = v.astype(BF16).astype(F32)
    lo = v - hi
    packed = jnp.where(lane < SSM_HEADS, hi, pltpu.roll(lo, SSM_HEADS, 1))
    return jnp.where(lane < 2 * SSM_HEADS, packed, 0.0).astype(BF16)


def _ssd_kernel(xbc_ref, z_ref, dt_ref, convw_ref, convb_ref, dtb_ref, alog_ref, dskip_ref,
                normw_ref, expand_ref, y_ref, ext_s, act_s, state_s):
    q = SSM_CHUNK
    lane = lax.broadcasted_iota(jnp.int32, (1, LANES), 1)

    @pl.when(pl.program_id(1) == 0)
    def _():
        ext_s[0:SUBLANES, :] = jnp.zeros((SUBLANES, XBC_DIM), F32)
        state_s[...] = jnp.zeros(state_s.shape, F32)

    ext_s[SUBLANES:SUBLANES + q, :] = xbc_ref[...].astype(F32)
    strip = 4 * LANES
    for c0 in range(0, XBC_DIM, strip):
        cols = slice(c0, c0 + strip)
        conv = convb_ref[:, cols]
        for t in range(SSM_CONV):
            r0 = SUBLANES - (SSM_CONV - 1) + t
            conv = conv + convw_ref[t:t + 1, cols] * ext_s[r0:r0 + q, cols]
        act_s[:, cols] = conv * _sigmoid(conv)
    ext_s[0:SUBLANES, :] = ext_s[q:q + SUBLANES, :]

    dt_in = dt_ref[...] + dtb_ref[...]
    dt = jnp.maximum(dt_in, 0.0) + jnp.log1p(jnp.exp(-jnp.abs(dt_in)))
    a_dt = dt * (-jnp.exp(alog_ref[...]))
    row = lax.broadcasted_iota(jnp.int32, (q, q), 0)
    col = lax.broadcasted_iota(jnp.int32, (q, q), 1)
    causal = col <= row
    tri = jnp.where(causal, 1.0, 0.0).astype(BF16)
    p1 = a_dt.astype(BF16)
    r1 = a_dt - p1.astype(F32)
    p2 = r1.astype(BF16)
    p3 = (r1 - p2.astype(F32)).astype(BF16)
    acs = _dot(tri, p1) + _dot(tri, p2) + _dot(tri, p3)
    acs_t = jnp.transpose(acs)
    exp_a = jnp.exp(acs)
    to_end = jnp.exp(acs[q - 1:q, :] - acs)

    expand = expand_ref[...]
    dt_x = _dot(_split_pack(dt, lane), expand)
    exp_a_x = _dot(_split_pack(exp_a, lane), expand)
    to_end_x = _dot(_split_pack(to_end, lane), expand)

    pair_lane = lane // SSM_HEAD_DIM
    for g in range(SSM_GROUPS):
        cg = slice(g * GROUP_WIDTH, (g + 1) * GROUP_WIDTH)
        xs = act_s[:, cg]
        xdt = xs * dt_x[:, cg]
        b_off = D_INNER + g * SSM_STATE
        c_off = D_INNER + SSM_GROUPS * SSM_STATE + g * SSM_STATE
        bmat = act_s[:, b_off:b_off + SSM_STATE]
        cmat = act_s[:, c_off:c_off + SSM_STATE].astype(BF16)
        cb = _dot_nt(cmat, bmat.astype(BF16))
        prev = state_s[g]
        y = _dot(cmat, prev.astype(BF16)) * exp_a_x[:, cg]
        y = y + xs * dskip_ref[:, cg]
        xdt_b = xdt.astype(BF16)
        diag = []
        for pr in range(HEADS_PER_GROUP // 2):
            xp = xdt_b[:, pr * LANES:(pr + 1) * LANES]
            outs = []
            for e in range(2):
                h = g * HEADS_PER_GROUP + 2 * pr + e
                seg = acs[:, h:h + 1] - acs_t[h:h + 1, :]
                decay = jnp.where(causal, jnp.exp(jnp.where(causal, seg, 0.0)), 0.0)
                outs.append(_dot((cb * decay).astype(BF16), xp))
            diag.append(jnp.where(pair_lane == 0, outs[0], outs[1]))
        y = y + jnp.concatenate(diag, axis=1)

        xw = (xdt * to_end_x[:, cg]).astype(BF16)
        new_state = _dot(jnp.transpose(bmat).astype(BF16), xw)
        state_s[g] = prev * exp_a_x[q - 1:q, cg] + new_state

        zg = z_ref[:, cg].astype(F32)
        hf = y * (zg * _sigmoid(zg))
        ms = jnp.mean(hf * hf, axis=1, keepdims=True)
        y_ref[:, cg] = (hf * lax.rsqrt(ms + RMS_EPS) * normw_ref[:, cg]).astype(y_ref.dtype)


def _ssd(proj2, dt2, bsz, seq, convw, convb, dtb, alog, dskip_x, normw, expand):
    nc = seq // SSM_CHUNK
    q = SSM_CHUNK
    rowblk = lambda b, c: b * nc + c
    const = lambda shape: pl.BlockSpec(shape, lambda b, c: (0, 0))
    return pl.pallas_call(
        _ssd_kernel,
        grid=(bsz, nc),
        in_specs=[pl.BlockSpec((q, XBC_DIM), lambda b, c: (rowblk(b, c), XBC_OFF // XBC_DIM)),
                  pl.BlockSpec((q, D_INNER), lambda b, c: (rowblk(b, c), Z_OFF // D_INNER)),
                  pl.BlockSpec((q, LANES), lambda b, c: (rowblk(b, c), 0)),
                  const((SSM_CONV, XBC_DIM)), const((1, XBC_DIM)), const((1, LANES)),
                  const((1, LANES)), const((1, D_INNER)), const((1, D_INNER)),
                  const((LANES, D_INNER))],
        out_specs=pl.BlockSpec((q, D_INNER), lambda b, c: (rowblk(b, c), 0)),
        out_shape=jax.ShapeDtypeStruct((bsz * seq, D_INNER), BF16),
        scratch_shapes=[pltpu.VMEM((q + SUBLANES, XBC_DIM), F32),
                        pltpu.VMEM((q, XBC_DIM), F32),
                        pltpu.VMEM((SSM_GROUPS, SSM_STATE, GROUP_WIDTH), F32)],
        compiler_params=_params(("parallel", "arbitrary")),
        name="ssd_scan",
    )(proj2, proj2, dt2, convw, convb, dtb, alog, dskip_x, normw, expand)


def _layer_norm(r, g, b):
    mu = jnp.mean(r, axis=1, keepdims=True)
    d = r - mu
    var = jnp.mean(d * d, axis=1, keepdims=True)
    return d * lax.rsqrt(var + LN_EPS) * g + b


def _merge_kernel(att_ref, y_ref, ga_ref, gs_ref, x_ref, wa_ref, ws_ref, wo_ref, g_ref, b_ref, o_ref):
    pa = _dot(att_ref[...], wa_ref[...])
    ps = _dot(y_ref[...], ws_ref[...])
    merged = _sigmoid(ga_ref[...].astype(F32)) * pa + _sigmoid(gs_ref[...].astype(F32)) * ps
    mix = _dot(merged.astype(BF16), wo_ref[...])
    o_ref[...] = _layer_norm(ALPHA * x_ref[...] + mix, g_ref[...], b_ref[...])


def _resident(shape):
    return pl.BlockSpec(shape, lambda i: (0, 0), pipeline_mode=pl.Buffered(1))


def _merge(att2, y2, proj2, x2, wa, ws, wo, g, b, tm):
    m = x2.shape[0]
    return pl.pallas_call(
        _merge_kernel,
        grid=(m // tm,),
        in_specs=[pl.BlockSpec((tm, ATT_WIDTH), lambda i: (i, 0)),
                  pl.BlockSpec((tm, D_INNER), lambda i: (i, 0)),
                  pl.BlockSpec((tm, D_MODEL), lambda i: (i, GA_OFF // D_MODEL)),
                  pl.BlockSpec((tm, D_MODEL), lambda i: (i, GS_OFF // D_MODEL)),
                  pl.BlockSpec((tm, D_MODEL), lambda i: (i, 0)),
                  _resident((ATT_WIDTH, D_MODEL)), _resident((D_INNER, D_MODEL)),
                  _resident((D_MODEL, D_MODEL)), _resident((1, D_MODEL)), _resident((1, D_MODEL))],
        out_specs=pl.BlockSpec((tm, D_MODEL), lambda i: (i, 0)),
        out_shape=jax.ShapeDtypeStruct((m, D_MODEL), F32),
        compiler_params=_params(("parallel",)),
        name="merge_ln",
    )(att2, y2, proj2, proj2, x2, wa, ws, wo, g, b)


def _mlp_kernel(h_ref, wu_ref, wd_ref, g_ref, b_ref, o_ref):
    h = h_ref[...]
    up = _dot(h.astype(BF16), wu_ref[...])
    act = jnp.square(jnp.maximum(up, 0.0)).astype(BF16)
    down = _dot(act, wd_ref[...])
    o_ref[...] = _layer_norm(ALPHA * h + down, g_ref[...], b_ref[...])


def _mlp(h2, wu, wd, g, b, tm):
    m = h2.shape[0]
    return pl.pallas_call(
        _mlp_kernel,
        grid=(m // tm,),
        in_specs=[pl.BlockSpec((tm, D_MODEL), lambda i: (i, 0)),
                  _resident((D_MODEL, D_FF)), _resident((D_FF, D_MODEL)),
                  _resident((1, D_MODEL)), _resident((1, D_MODEL))],
        out_specs=pl.BlockSpec((tm, D_MODEL), lambda i: (i, 0)),
        out_shape=jax.ShapeDtypeStruct((m, D_MODEL), F32),
        compiler_params=_params(("parallel",)),
        name="mlp_ln",
    )(h2, wu, wd, g, b)


def _pad_lanes(v):
    return jnp.pad(v.astype(F32), (0, LANES - v.shape[0])).reshape(1, LANES)


def _rope_lane_freqs():
    inv_freq = ROPE_THETA ** (-jnp.arange(0, ROT_DIM, 2, dtype=F32) / ROT_DIM)
    per_head = jnp.concatenate([inv_freq, inv_freq, jnp.zeros((ATT_HEAD_DIM - ROT_DIM,), F32)])
    return jnp.tile(per_head, LANES // ATT_HEAD_DIM).reshape(1, LANES)


def _head_expand_matrix():
    r = jnp.arange(LANES)[:, None]
    c = jnp.arange(D_INNER)[None, :]
    return (((r % SSM_HEADS) == (c // SSM_HEAD_DIM)) & (r < 2 * SSM_HEADS)).astype(BF16)


def kernel(x, positions, ln1_g, ln1_b, w_in, conv_w, conv_b, dt_bias, a_log, d_skip, ssm_norm_w,
           w_attn_proj, w_ssm_proj, w_out, ln2_g, ln2_b, w_up, w_down):
    bsz, seq, d_model = x.shape
    assert d_model == D_MODEL and seq % MOBA_BLOCK == 0 and seq % SSM_CHUNK == 0
    assert w_in.shape == (DEPTH, D_MODEL, sum(IN_SPLITS))
    rows = bsz * seq
    tm_proj = 1024 if rows % 1024 == 0 else SSM_CHUNK
    tm_row = 512 if rows % 512 == 0 else SSM_CHUNK

    pos3 = positions.reshape(bsz, seq, 1)
    invf = _rope_lane_freqs()
    expand = _head_expand_matrix()

    x2 = x.reshape(rows, D_MODEL)
    for i in range(DEPTH):
        wq, wk, wv, wz, wxbc, wdt, wga, wgs = jnp.split(
            w_in[i], [int(sum(IN_SPLITS[:n])) for n in range(1, len(IN_SPLITS))], axis=1)
        w_cat = jnp.concatenate([wxbc, wz, wq, wk, wv, wga, wgs], axis=1).astype(BF16)
        w_dt = jnp.pad(wdt, ((0, 0), (0, LANES - SSM_HEADS))).astype(BF16)

        proj = _matmul(x2, w_cat, BF16, tm_proj, 512)
        dt_raw = _matmul(x2, w_dt, F32, tm_proj, LANES)

        att = _attention(proj.reshape(bsz, seq, PROJ_WIDTH), pos3, invf)
        y_ssm = _ssd(proj, dt_raw, bsz, seq, conv_w[i], conv_b[i].reshape(1, XBC_DIM),
                     _pad_lanes(dt_bias[i]), _pad_lanes(a_log[i]),
                     jnp.repeat(d_skip[i].astype(F32), SSM_HEAD_DIM).reshape(1, D_INNER),
                     ssm_norm_w[i].reshape(1, D_INNER), expand)
        h = _merge(att.reshape(rows, ATT_WIDTH), y_ssm, proj, x2,
                   w_attn_proj[i].astype(BF16), w_ssm_proj[i].astype(BF16), w_out[i].astype(BF16),
                   ln1_g[i].reshape(1, D_MODEL), ln1_b[i].reshape(1, D_MODEL), tm_row)
        x2 = _mlp(h, w_up[i].astype(BF16), w_down[i].astype(BF16),
                  ln2_g[i].reshape(1, D_MODEL), ln2_b[i].reshape(1, D_MODEL), tm_row)
    return x2.reshape(bsz, seq, D_MODEL)
```

```python
import functools

import jax
import jax.numpy as jnp
from jax import lax
from jax.experimental import pallas as pl
from jax.experimental.pallas import tpu as pltpu

F32 = jnp.float32
BF16 = jnp.bfloat16

D_MODEL = 1024
DEPTH = 2
ATT_HEAD_DIM = 64
ATT_HEADS = D_MODEL // ATT_HEAD_DIM
ATT_WIDTH = ATT_HEADS * ATT_HEAD_DIM
ROT_DIM = ATT_HEAD_DIM // 4
ROT_HALF = ROT_DIM // 2
ROPE_THETA = 500000.0
MOBA_BLOCK = 256
MOBA_TOPK = 3
D_INNER = 2 * D_MODEL
SSM_HEAD_DIM = 64
SSM_HEADS = D_INNER // SSM_HEAD_DIM
SSM_GROUPS = 8
SSM_STATE = 128
SSM_CONV = 4
SSM_CHUNK = 256
XBC_DIM = D_INNER + 2 * SSM_GROUPS * SSM_STATE
D_FF = 4 * D_MODEL
ALPHA = (2 * DEPTH) ** 0.25
LN_EPS = 1e-5
RMS_EPS = 1e-5
IN_SPLITS = (ATT_WIDTH, ATT_WIDTH, ATT_WIDTH, D_INNER, XBC_DIM, SSM_HEADS, D_MODEL, D_MODEL)

LANES = 128
SUBLANES = 8
VMEM_LIMIT = 56 * 1024 * 1024

XBC_OFF = 0
Z_OFF = XBC_OFF + XBC_DIM
Q_OFF = Z_OFF + D_INNER
K_OFF = Q_OFF + ATT_WIDTH
V_OFF = K_OFF + ATT_WIDTH
GA_OFF = V_OFF + ATT_WIDTH
GS_OFF = GA_OFF + D_MODEL
PROJ_WIDTH = GS_OFF + D_MODEL

HEADS_PER_GROUP = SSM_HEADS // SSM_GROUPS
GROUP_WIDTH = HEADS_PER_GROUP * SSM_HEAD_DIM
NEG_INF = float("-inf")
MASK_BIAS = -1e30


def _params(semantics):
    return pltpu.CompilerParams(dimension_semantics=semantics, vmem_limit_bytes=VMEM_LIMIT)


def _dot(a, b):
    return jnp.dot(a, b, preferred_element_type=F32)


def _dot_nt(a, b, precision=None):
    return lax.dot_general(a, b, (((1,), (1,)), ((), ())), precision=precision,
                           preferred_element_type=F32)


def _sigmoid(x):
    return 1.0 / (1.0 + jnp.exp(-x))


def _matmul_kernel(x_ref, w_ref, o_ref, xb_ref):
    @pl.when(pl.program_id(1) == 0)
    def _():
        xb_ref[...] = x_ref[...].astype(BF16)

    o_ref[...] = _dot(xb_ref[...], w_ref[...]).astype(o_ref.dtype)


def _matmul(x, w, out_dtype, tm, tn):
    m, k = x.shape
    n = w.shape[1]
    return pl.pallas_call(
        _matmul_kernel,
        grid=(m // tm, n // tn),
        in_specs=[pl.BlockSpec((tm, k), lambda i, j: (i, 0)),
                  pl.BlockSpec((k, tn), lambda i, j: (0, j))],
        out_specs=pl.BlockSpec((tm, tn), lambda i, j: (i, j)),
        out_shape=jax.ShapeDtypeStruct((m, n), out_dtype),
        scratch_shapes=[pltpu.VMEM((tm, k), BF16)],
        compiler_params=_params(("parallel", "arbitrary")),
        name="proj_matmul",
    )(x, w)


def _attn_kernel(pos_ref, invf_ref, q_ref, k_ref, v_ref, o_ref,
                 cos_s, sa_s, sb_s, k_s, vt_s, rhs_s, s_s):
    seq = q_ref.shape[1]
    nblk = seq // MOBA_BLOCK
    blk = MOBA_BLOCK
    lane = lax.broadcasted_iota(jnp.int32, (1, LANES), 1)
    head_of_lane = lane // ATT_HEAD_DIM
    dim_of_lane = lane % ATT_HEAD_DIM

    @pl.when(pl.program_id(1) == 0)
    def _():
        ang = pos_ref[0].astype(F32) * invf_ref[...]
        cos = jnp.cos(ang)
        sin = jnp.sin(ang)
        cos_s[...] = cos
        sa_s[...] = jnp.where(dim_of_lane < ROT_HALF, -sin, 0.0)
        sb_s[...] = jnp.where((dim_of_lane >= ROT_HALF) & (dim_of_lane < ROT_DIM), sin, 0.0)
        blk_of_row = lax.broadcasted_iota(jnp.int32, (seq, LANES), 0) // blk
        k_s[:, LANES:] = jnp.where(blk_of_row == lane, 1.0, 0.0).astype(BF16)

    def rope(t, rows):
        return (t * cos_s[rows, :]
                + pltpu.roll(t, LANES - ROT_HALF, 1) * sa_s[rows, :]
                + pltpu.roll(t, ROT_HALF, 1) * sb_s[rows, :])

    kmeans = []
    for j in range(nblk):
        rows = pl.ds(j * blk, blk)
        kr = rope(k_ref[0, rows, :].astype(F32), rows)
        k_s[rows, 0:LANES] = kr.astype(BF16)
        kmeans.append(jnp.sum(kr, axis=0, keepdims=True) * (1.0 / blk))
        vt_s[j] = jnp.transpose(v_ref[0, rows, :].astype(F32)).astype(BF16)
    kmean = jnp.concatenate(kmeans, axis=0)

    row_n = lax.broadcasted_iota(jnp.int32, (nblk, blk), 0)
    head_of_row = lax.broadcasted_iota(jnp.int32, (LANES, 1), 0) // ATT_HEAD_DIM
    scale = ATT_HEAD_DIM ** -0.5
    bias_rows = 2 * SUBLANES
    for qb in range(nblk):
        rows = pl.ds(qb * blk, blk)
        qr = rope(q_ref[0, rows, :].astype(F32), rows)
        qt = jnp.transpose(qr * scale)
        for h in range(2):
            past = jnp.where(row_n < qb, 1.0, 0.0)
            if qb <= MOBA_TOPK:
                sel = past
            else:
                kmh = jnp.where(head_of_lane == h, kmean, 0.0)
                gate = _dot_nt(kmh, qr, precision=lax.Precision.HIGHEST)
                cnt = jnp.zeros((nblk, blk), F32)
                for m in range(qb):
                    gm = gate[m:m + 1, :]
                    ge = jnp.where(gm >= gate, 1.0, 0.0)
                    gt = jnp.where(gm > gate, 1.0, 0.0)
                    cnt = cnt + jnp.where(row_n > m, ge, jnp.where(row_n < m, gt, 0.0))
                sel = jnp.where(cnt < MOBA_TOPK, past, 0.0)
            attended = jnp.where(row_n == qb, 1.0, sel)
            bias = jnp.where(attended > 0.5, 0.0, MASK_BIAS)
            rhs_s[h, qb, 0:LANES, :] = jnp.where(head_of_row == h, qt, 0.0).astype(BF16)
            rhs_s[h, qb, LANES:LANES + bias_rows, :] = jnp.concatenate(
                [bias, jnp.zeros((bias_rows - nblk, blk), F32)], axis=0).astype(BF16)
            rhs_s[h, qb, LANES + bias_rows:, :] = jnp.zeros((LANES - bias_rows, blk), BF16)

    key_row = lax.broadcasted_iota(jnp.int32, (blk, blk), 0)
    query_col = lax.broadcasted_iota(jnp.int32, (blk, blk), 1)
    causal = key_row <= query_col
    hd = ATT_HEAD_DIM

    for qb in range(nblk):
        outs = []
        for h in range(2):
            rhs = rhs_s[h, qb]
            m = None
            for j in range(qb + 1):
                s = _dot(k_s[j * blk:(j + 1) * blk, :], rhs)
                if j == qb:
                    s = jnp.where(causal, s, NEG_INF)
                s_s[h, j] = s
                mj = jnp.max(s, axis=0, keepdims=True)
                m = mj if m is None else jnp.maximum(m, mj)
            l = jnp.zeros((1, blk), F32)
            acc = jnp.zeros((hd, blk), F32)
            for j in range(qb + 1):
                p = jnp.exp(s_s[h, j] - m)
                l = l + jnp.sum(p, axis=0, keepdims=True)
                acc = acc + _dot(vt_s[j, h * hd:(h + 1) * hd, :], p.astype(BF16))
            outs.append(acc / l)
        out_t = jnp.concatenate(outs, axis=0)
        o_ref[0, qb * blk:(qb + 1) * blk, :] = jnp.transpose(out_t).astype(o_ref.dtype)


def _attention(proj3, pos3, invf):
    bsz, seq, _ = proj3.shape
    npairs = ATT_HEADS // 2
    nblk = seq // MOBA_BLOCK
    qkv_spec = lambda off: pl.BlockSpec((1, seq, LANES), lambda b, p: (b, 0, off // LANES + p))
    return pl.pallas_call(
        _attn_kernel,
        grid=(bsz, npairs),
        in_specs=[pl.BlockSpec((1, seq, 1), lambda b, p: (b, 0, 0)),
                  pl.BlockSpec((1, LANES), lambda b, p: (0, 0)),
                  qkv_spec(Q_OFF), qkv_spec(K_OFF), qkv_spec(V_OFF)],
        out_specs=pl.BlockSpec((1, seq, LANES), lambda b, p: (b, 0, p)),
        out_shape=jax.ShapeDtypeStruct((bsz, seq, ATT_WIDTH), BF16),
        scratch_shapes=[pltpu.VMEM((seq, LANES), F32), pltpu.VMEM((seq, LANES), F32),
                        pltpu.VMEM((seq, LANES), F32),
                        pltpu.VMEM((seq, 2 * LANES), BF16),
                        pltpu.VMEM((nblk, LANES, MOBA_BLOCK), BF16),
                        pltpu.VMEM((2, nblk, 2 * LANES, MOBA_BLOCK), BF16),
                        pltpu.VMEM((2, nblk, MOBA_BLOCK, MOBA_BLOCK), F32)],
        compiler_params=_params(("parallel", "arbitrary")),
        name="moba_attention",
    )(pos3, invf, proj3, proj3, proj3)


def _split_pack(v, lane):
    hi = v.astype(BF16).astype(F32)
    lo = v - hi
    packed = jnp.where(lane < SSM_HEADS, hi, pltpu.roll(lo, SSM_HEADS, 1))
    return jnp.where(lane < 2 * SSM_HEADS, packed, 0.0).astype(BF16)


def _ssd_kernel(xbc_ref, z_ref, dt_ref, convw_ref, convb_ref, dtb_ref, alog_ref, dskip_ref,
                normw_ref, expand_ref, y_ref, ext_s, act_s, state_s):
    q = SSM_CHUNK
    lane = lax.broadcasted_iota(jnp.int32, (1, LANES), 1)

    @pl.when(pl.program_id(1) == 0)
    def _():
        ext_s[0:SUBLANES, :] = jnp.zeros((SUBLANES, XBC_DIM), F32)
        state_s[...] = jnp.zeros(state_s.shape, F32)

    ext_s[SUBLANES:SUBLANES + q, :] = xbc_ref[...].astype(F32)
    strip = 4 * LANES
    for c0 in range(0, XBC_DIM, strip):
        cols = slice(c0, c0 + strip)
        conv = convb_ref[:, cols]
        for t in range(SSM_CONV):
            r0 = SUBLANES - (SSM_CONV - 1) + t
            conv = conv + convw_ref[t:t + 1, cols] * ext_s[r0:r0 + q, cols]
        act_s[:, cols] = conv * _sigmoid(conv)
    ext_s[0:SUBLANES, :] = ext_s[q:q + SUBLANES, :]

    dt_in = dt_ref[...] + dtb_ref[...]
    dt = jnp.maximum(dt_in, 0.0) + jnp.log1p(jnp.exp(-jnp.abs(dt_in)))
    a_dt = dt * (-jnp.exp(alog_ref[...]))
    row = lax.broadcasted_iota(jnp.int32, (q, q), 0)
    col = lax.broadcasted_iota(jnp.int32, (q, q), 1)
    causal = col <= row
    tri = jnp.where(causal, 1.0, 0.0).astype(BF16)
    p1 = a_dt.astype(BF16)
    r1 = a_dt - p1.astype(F32)
    p2 = r1.astype(BF16)
    p3 = (r1 - p2.astype(F32)).astype(BF16)
    acs = _dot(tri, p1) + _dot(tri, p2) + _dot(tri, p3)
    acs_t = jnp.transpose(acs)
    exp_a = jnp.exp(acs)
    to_end = jnp.exp(acs[q - 1:q, :] - acs)

    expand = expand_ref[...]
    dt_x = _dot(_split_pack(dt, lane), expand)
    exp_a_x = _dot(_split_pack(exp_a, lane), expand)
    to_end_x = _dot(_split_pack(to_end, lane), expand)

    pair_lane = lane // SSM_HEAD_DIM
    for g in range(SSM_GROUPS):
        cg = slice(g * GROUP_WIDTH, (g + 1) * GROUP_WIDTH)
        xs = act_s[:, cg]
        xdt = xs * dt_x[:, cg]
        b_off = D_INNER + g * SSM_STATE
        c_off = D_INNER + SSM_GROUPS * SSM_STATE + g * SSM_STATE
        bmat = act_s[:, b_off:b_off + SSM_STATE]
        cmat = act_s[:, c_off:c_off + SSM_STATE].astype(BF16)
        cb = _dot_nt(cmat, bmat.astype(BF16))
        prev = state_s[g]
        y = _dot(cmat, prev.astype(BF16)) * exp_a_x[:, cg]
        y = y + xs * dskip_ref[:, cg]
        xdt_b = xdt.astype(BF16)
        diag = []
        for pr in range(HEADS_PER_GROUP // 2):
            xp = xdt_b[:, pr * LANES:(pr + 1) * LANES]
            outs = []
            for e in range(2):
                h = g * HEADS_PER_GROUP + 2 * pr + e
                seg = acs[:, h:h + 1] - acs_t[h:h + 1, :]
                decay = jnp.where(causal, jnp.exp(jnp.where(causal, seg, 0.0)), 0.0)
                outs.append(_dot((cb * decay).astype(BF16), xp))
            diag.append(jnp.where(pair_lane == 0, outs[0], outs[1]))
        y = y + jnp.concatenate(diag, axis=1)

        xw = (xdt * to_end_x[:, cg]).astype(BF16)
        new_state = _dot(jnp.transpose(bmat).astype(BF16), xw)
        state_s[g] = prev * exp_a_x[q - 1:q, cg] + new_state

        zg = z_ref[:, cg].astype(F32)
        hf = y * (zg * _sigmoid(zg))
        ms = jnp.mean(hf * hf, axis=1, keepdims=True)
        y_ref[:, cg] = (hf * lax.rsqrt(ms + RMS_EPS) * normw_ref[:, cg]).astype(y_ref.dtype)


def _ssd(proj2, dt2, bsz, seq, convw, convb, dtb, alog, dskip_x, normw, expand):
    nc = seq // SSM_CHUNK
    q = SSM_CHUNK
    rowblk = lambda b, c: b * nc + c
    const = lambda shape: pl.BlockSpec(shape, lambda b, c: (0, 0))
    return pl.pallas_call(
        _ssd_kernel,
        grid=(bsz, nc),
        in_specs=[pl.BlockSpec((q, XBC_DIM), lambda b, c: (rowblk(b, c), XBC_OFF // XBC_DIM)),
                  pl.BlockSpec((q, D_INNER), lambda b, c: (rowblk(b, c), Z_OFF // D_INNER)),
                  pl.BlockSpec((q, LANES), lambda b, c: (rowblk(b, c), 0)),
                  const((SSM_CONV, XBC_DIM)), const((1, XBC_DIM)), const((1, LANES)),
                  const((1, LANES)), const((1, D_INNER)), const((1, D_INNER)),
                  const((LANES, D_INNER))],
        out_specs=pl.BlockSpec((q, D_INNER), lambda b, c: (rowblk(b, c), 0)),
        out_shape=jax.ShapeDtypeStruct((bsz * seq, D_INNER), BF16),
        scratch_shapes=[pltpu.VMEM((q + SUBLANES, XBC_DIM), F32),
                        pltpu.VMEM((q, XBC_DIM), F32),
                        pltpu.VMEM((SSM_GROUPS, SSM_STATE, GROUP_WIDTH), F32)],
        compiler_params=_params(("parallel", "arbitrary")),
        name="ssd_scan",
    )(proj2, proj2, dt2, convw, convb, dtb, alog, dskip_x, normw, expand)


def _layer_norm(r, g, b):
    mu = jnp.mean(r, axis=1, keepdims=True)
    d = r - mu
    var = jnp.mean(d * d, axis=1, keepdims=True)
    return d * lax.rsqrt(var + LN_EPS) * g + b


def _merge_kernel(att_ref, y_ref, ga_ref, gs_ref, x_ref, wa_ref, ws_ref, wo_ref, g_ref, b_ref, o_ref):
    pa = _dot(att_ref[...], wa_ref[...])
    ps = _dot(y_ref[...], ws_ref[...])
    merged = _sigmoid(ga_ref[...].astype(F32)) * pa + _sigmoid(gs_ref[...].astype(F32)) * ps
    mix = _dot(merged.astype(BF16), wo_ref[...])
    o_ref[...] = _layer_norm(ALPHA * x_ref[...] + mix, g_ref[...], b_ref[...])


def _resident(shape):
    return pl.BlockSpec(shape, lambda i: (0, 0), pipeline_mode=pl.Buffered(1))


def _merge(att2, y2, proj2, x2, wa, ws, wo, g, b, tm):
    m = x2.shape[0]
    return pl.pallas_call(
        _merge_kernel,
        grid=(m // tm,),
        in_specs=[pl.BlockSpec((tm, ATT_WIDTH), lambda i: (i, 0)),
                  pl.BlockSpec((tm, D_INNER), lambda i: (i, 0)),
                  pl.BlockSpec((tm, D_MODEL), lambda i: (i, GA_OFF // D_MODEL)),
                  pl.BlockSpec((tm, D_MODEL), lambda i: (i, GS_OFF // D_MODEL)),
                  pl.BlockSpec((tm, D_MODEL), lambda i: (i, 0)),
                  _resident((ATT_WIDTH, D_MODEL)), _resident((D_INNER, D_MODEL)),
                  _resident((D_MODEL, D_MODEL)), _resident((1, D_MODEL)), _resident((1, D_MODEL))],
        out_specs=pl.BlockSpec((tm, D_MODEL), lambda i: (i, 0)),
        out_shape=jax.ShapeDtypeStruct((m, D_MODEL), F32),
        compiler_params=_params(("parallel",)),
        name="merge_ln",
    )(att2, y2, proj2, proj2, x2, wa, ws, wo, g, b)


def _mlp_kernel(h_ref, wu_ref, wd_ref, g_ref, b_ref, o_ref):
    h = h_ref[...]
    up = _dot(h.astype(BF16), wu_ref[...])
    act = jnp.square(jnp.maximum(up, 0.0)).astype(BF16)
    down = _dot(act, wd_ref[...])
    o_ref[...] = _layer_norm(ALPHA * h + down, g_ref[...], b_ref[...])


def _mlp(h2, wu, wd, g, b, tm):
    m = h2.shape[0]
    return pl.pallas_call(
        _mlp_kernel,
        grid=(m // tm,),
        in_specs=[pl.BlockSpec((tm, D_MODEL), lambda i: (i, 0)),
                  _resident((D_MODEL, D_FF)), _resident((D_FF, D_MODEL)),
                  _resident((1, D_MODEL)), _resident((1, D_MODEL))],
        out_specs=pl.BlockSpec((tm, D_MODEL), lambda i: (i, 0)),
        out_shape=jax.ShapeDtypeStruct((m, D_MODEL), F32),
        compiler_params=_params(("parallel",)),
        name="mlp_ln",
    )(h2, wu, wd, g, b)


def _pad_lanes(v):
    return jnp.pad(v.astype(F32), (0, LANES - v.shape[0])).reshape(1, LANES)


def _rope_lane_freqs():
    inv_freq = ROPE_THETA ** (-jnp.arange(0, ROT_DIM, 2, dtype=F32) / ROT_DIM)
    per_head = jnp.concatenate([inv_freq, inv_freq, jnp.zeros((ATT_HEAD_DIM - ROT_DIM,), F32)])
    return jnp.tile(per_head, LANES // ATT_HEAD_DIM).reshape(1, LANES)


def _head_expand_matrix():
    r = jnp.arange(LANES)[:, None]
    c = jnp.arange(D_INNER)[None, :]
    return (((r % SSM_HEADS) == (c // SSM_HEAD_DIM)) & (r < 2 * SSM_HEADS)).astype(BF16)


def kernel(x, positions, ln1_g, ln1_b, w_in, conv_w, conv_b, dt_bias, a_log, d_skip, ssm_norm_w,
           w_attn_proj, w_ssm_proj, w_out, ln2_g, ln2_b, w_up, w_down):
    bsz, seq, d_model = x.shape
    assert d_model == D_MODEL and seq % MOBA_BLOCK == 0 and seq % SSM_CHUNK == 0
    assert w_in.shape == (DEPTH, D_MODEL, sum(IN_SPLITS))
    rows = bsz * seq
    tm_proj = 1024 if rows % 1024 == 0 else SSM_CHUNK
    tm_row = 512 if rows % 512 == 0 else SSM_CHUNK

    pos3 = positions.reshape(bsz, seq, 1)
    invf = _rope_lane_freqs()
    expand = _head_expand_matrix()

    x2 = x.reshape(rows, D_MODEL)
    for i in range(DEPTH):
        wq, wk, wv, wz, wxbc, wdt, wga, wgs = jnp.split(
            w_in[i], [int(sum(IN_SPLITS[:n])) for n in range(1, len(IN_SPLITS))], axis=1)
        w_cat = jnp.concatenate([wxbc, wz, wq, wk, wv, wga, wgs], axis=1).astype(BF16)
        w_dt = jnp.pad(wdt, ((0, 0), (0, LANES - SSM_HEADS))).astype(BF16)

        proj = _matmul(x2, w_cat, BF16, tm_proj, 512)
        dt_raw = _matmul(x2, w_dt, F32, tm_proj, LANES)

        att = _attention(proj.reshape(bsz, seq, PROJ_WIDTH), pos3, invf)
        y_ssm = _ssd(proj, dt_raw, bsz, seq, conv_w[i], conv_b[i].reshape(1, XBC_DIM),
                     _pad_lanes(dt_bias[i]), _pad_lanes(a_log[i]),
                     jnp.repeat(d_skip[i].astype(F32), SSM_HEAD_DIM).reshape(1, D_INNER),
                     ssm_norm_w[i].reshape(1, D_INNER), expand)
        h = _merge(att.reshape(rows, ATT_WIDTH), y_ssm, proj, x2,
                   w_attn_proj[i].astype(BF16), w_ssm_proj[i].astype(BF16), w_out[i].astype(BF16),
                   ln1_g[i].reshape(1, D_MODEL), ln1_b[i].reshape(1, D_MODEL), tm_row)
        x2 = _mlp(h, w_up[i].astype(BF16), w_down[i].astype(BF16),
                  ln2_g[i].reshape(1, D_MODEL), ln2_b[i].reshape(1, D_MODEL), tm_row)
    return x2.reshape(bsz, seq, D_MODEL)
```

```python
import functools

import jax
import jax.numpy as jnp
from jax import lax
from jax.experimental import pallas as pl
from jax.experimental.pallas import tpu as pltpu

F32 = jnp.float32
BF16 = jnp.bfloat16

D_MODEL = 1024
DEPTH = 2
ATT_HEAD_DIM = 64
ATT_HEADS = D_MODEL // ATT_HEAD_DIM
ATT_WIDTH = ATT_HEADS * ATT_HEAD_DIM
ROT_DIM = ATT_HEAD_DIM // 4
ROT_HALF = ROT_DIM // 2
ROPE_THETA = 500000.0
MOBA_BLOCK = 256
MOBA_TOPK = 3
D_INNER = 2 * D_MODEL
SSM_HEAD_DIM = 64
SSM_HEADS = D_INNER // SSM_HEAD_DIM
SSM_GROUPS = 8
SSM_STATE = 128
SSM_CONV = 4
SSM_CHUNK = 256
XBC_DIM = D_INNER + 2 * SSM_GROUPS * SSM_STATE
D_FF = 4 * D_MODEL
ALPHA = (2 * DEPTH) ** 0.25
LN_EPS = 1e-5
RMS_EPS = 1e-5
IN_SPLITS = (ATT_WIDTH, ATT_WIDTH, ATT_WIDTH, D_INNER, XBC_DIM, SSM_HEADS, D_MODEL, D_MODEL)

LANES = 128
MXU_WIDTH = 256
SUBLANES = 8
VMEM_LIMIT = 56 * 1024 * 1024

XBC_OFF = 0
Z_OFF = XBC_OFF + XBC_DIM
Q_OFF = Z_OFF + D_INNER
K_OFF = Q_OFF + ATT_WIDTH
V_OFF = K_OFF + ATT_WIDTH
GA_OFF = V_OFF + ATT_WIDTH
GS_OFF = GA_OFF + D_MODEL
PROJ_WIDTH = GS_OFF + D_MODEL

HEADS_PER_GROUP = SSM_HEADS // SSM_GROUPS
GROUP_WIDTH = HEADS_PER_GROUP * SSM_HEAD_DIM
NEG_INF = float("-inf")
MASK_BIAS = -1e30
LOG2E = 1.4426950408889634


def _params(semantics):
    return pltpu.CompilerParams(dimension_semantics=semantics, vmem_limit_bytes=VMEM_LIMIT)


def _dot(a, b):
    return jnp.dot(a, b, preferred_element_type=F32)


def _dot_nt(a, b, precision=None):
    return lax.dot_general(a, b, (((1,), (1,)), ((), ())), precision=precision,
                           preferred_element_type=F32)


def _sigmoid(x):
    return 1.0 / (1.0 + jnp.exp2(x * (-LOG2E)))


def _proj_kernel(x_ref, w_ref, wdt_ref, o_ref, dt_ref, xb_ref):
    @pl.when(pl.program_id(1) == 0)
    def _():
        xb_ref[...] = x_ref[...].astype(BF16)
        dt_ref[...] = _dot(xb_ref[...], wdt_ref[...])

    for c0 in range(0, o_ref.shape[1], MXU_WIDTH):
        cols = slice(c0, min(c0 + MXU_WIDTH, o_ref.shape[1]))
        o_ref[:, cols] = _dot(xb_ref[...], w_ref[:, cols]).astype(o_ref.dtype)


def _projection(x, w, w_dt, tm, tn):
    m, k = x.shape
    n = w.shape[1]
    return pl.pallas_call(
        _proj_kernel,
        grid=(m // tm, n // tn),
        in_specs=[pl.BlockSpec((tm, k), lambda i, j: (i, 0)),
                  pl.BlockSpec((k, tn), lambda i, j: (0, j)),
                  pl.BlockSpec((k, LANES), lambda i, j: (0, 0))],
        out_specs=[pl.BlockSpec((tm, tn), lambda i, j: (i, j)),
                   pl.BlockSpec((tm, LANES), lambda i, j: (i, 0))],
        out_shape=[jax.ShapeDtypeStruct((m, n), BF16), jax.ShapeDtypeStruct((m, LANES), F32)],
        scratch_shapes=[pltpu.VMEM((tm, k), BF16)],
        compiler_params=_params(("parallel", "arbitrary")),
        name="proj_matmul",
    )(x, w, w_dt)


def _attn_kernel(pos_ref, invf_ref, q_ref, k_ref, v_ref, o_ref,
                 cos_s, sa_s, sb_s, k_s, vt_s, rhs_s, s_s):
    seq = q_ref.shape[1]
    nblk = seq // MOBA_BLOCK
    blk = MOBA_BLOCK
    lane = lax.broadcasted_iota(jnp.int32, (1, LANES), 1)
    head_of_lane = lane // ATT_HEAD_DIM
    dim_of_lane = lane % ATT_HEAD_DIM

    @pl.when(pl.program_id(1) == 0)
    def _():
        ang = pos_ref[0].astype(F32) * invf_ref[...]
        cos = jnp.cos(ang)
        sin = jnp.sin(ang)
        cos_s[...] = cos
        sa_s[...] = jnp.where(dim_of_lane < ROT_HALF, -sin, 0.0)
        sb_s[...] = jnp.where((dim_of_lane >= ROT_HALF) & (dim_of_lane < ROT_DIM), sin, 0.0)
        blk_of_row = lax.broadcasted_iota(jnp.int32, (seq, LANES), 0) // blk
        k_s[:, LANES:] = jnp.where(blk_of_row == lane, 1.0, 0.0).astype(BF16)

    def rope(t, rows):
        return (t * cos_s[rows, :]
                + pltpu.roll(t, LANES - ROT_HALF, 1) * sa_s[rows, :]
                + pltpu.roll(t, ROT_HALF, 1) * sb_s[rows, :])

    hd = ATT_HEAD_DIM
    extra = vt_s.shape[2] - hd
    ones_row = jnp.where(lax.broadcasted_iota(jnp.int32, (extra, blk), 0) == 0, 1.0, 0.0).astype(BF16)
    kmeans = []
    for j in range(nblk):
        rows = pl.ds(j * blk, blk)
        kr = rope(k_ref[0, rows, :].astype(F32), rows)
        k_s[rows, 0:LANES] = kr.astype(BF16)
        kmeans.append(jnp.sum(kr, axis=0, keepdims=True) * (1.0 / blk))
        vt = jnp.transpose(v_ref[0, rows, :].astype(F32)).astype(BF16)
        for h in range(2):
            vt_s[j, h, 0:hd, :] = vt[h * hd:(h + 1) * hd, :]
            vt_s[j, h, hd:, :] = ones_row
    kmean = jnp.concatenate(kmeans, axis=0)

    row_n = lax.broadcasted_iota(jnp.int32, (nblk, blk), 0)
    head_of_row = lax.broadcasted_iota(jnp.int32, (LANES, 1), 0) // ATT_HEAD_DIM
    scale = ATT_HEAD_DIM ** -0.5 * LOG2E
    bias_rows = 2 * SUBLANES
    for qb in range(nblk):
        rows = pl.ds(qb * blk, blk)
        qr = rope(q_ref[0, rows, :].astype(F32), rows)
        qt = jnp.transpose(qr * scale)
        for h in range(2):
            past = jnp.where(row_n < qb, 1.0, 0.0)
            if qb <= MOBA_TOPK:
                sel = past
            else:
                kmh = jnp.where(head_of_lane == h, kmean, 0.0)
                gate = _dot_nt(kmh, qr, precision=lax.Precision.HIGHEST)
                cnt = jnp.zeros((nblk, blk), F32)
                for m in range(qb):
                    gm = gate[m:m + 1, :]
                    ge = jnp.where(gm >= gate, 1.0, 0.0)
                    gt = jnp.where(gm > gate, 1.0, 0.0)
                    cnt = cnt + jnp.where(row_n > m, ge, jnp.where(row_n < m, gt, 0.0))
                sel = jnp.where(cnt < MOBA_TOPK, past, 0.0)
            attended = jnp.where(row_n == qb, 1.0, sel)
            bias = jnp.where(attended > 0.5, 0.0, MASK_BIAS)
            rhs_s[h, qb, 0:LANES, :] = jnp.where(head_of_row == h, qt, 0.0).astype(BF16)
            rhs_s[h, qb, LANES:LANES + bias_rows, :] = jnp.concatenate(
                [bias, jnp.zeros((bias_rows - nblk, blk), F32)], axis=0).astype(BF16)
            rhs_s[h, qb, LANES + bias_rows:, :] = jnp.zeros((LANES - bias_rows, blk), BF16)

    key_row = lax.broadcasted_iota(jnp.int32, (blk, blk), 0)
    query_col = lax.broadcasted_iota(jnp.int32, (blk, blk), 1)
    causal = key_row <= query_col

    def score_tile(qb, h, j):
        s = _dot(k_s[j * blk:(j + 1) * blk, :], rhs_s[h, qb])
        if j == qb:
            s = jnp.where(causal, s, NEG_INF)
        s_s[h, j] = s
        return jnp.max(s, axis=0, keepdims=True)

    def value_tile(h, j, m):
        p = jnp.exp2(s_s[h, j] - m)
        return _dot(vt_s[j, h], p.astype(BF16))

    def fold_max(m, mj):
        return mj if m is None else jnp.maximum(m, mj)

    units = [(qb, h) for qb in range(nblk) for h in range(2)]
    m_next = None
    for j in range(units[0][0] + 1):
        m_next = fold_max(m_next, score_tile(*units[0], j))
    outs = []
    for i, (qb, h) in enumerate(units):
        m, m_next = m_next, None
        nxt = units[i + 1] if i + 1 < len(units) else None
        todo_next = list(range(nxt[0] + 1)) if nxt else []
        acc = None
        for j in range(qb + 1):
            if todo_next:
                m_next = fold_max(m_next, score_tile(*nxt, todo_next.pop(0)))
            pv = value_tile(h, j, m)
            acc = pv if acc is None else acc + pv
        while todo_next:
            m_next = fold_max(m_next, score_tile(*nxt, todo_next.pop(0)))
        outs.append(acc[0:hd, :] / acc[hd:hd + 1, :])
        if h == 1:
            out_t = jnp.concatenate(outs, axis=0)
            o_ref[0, qb * blk:(qb + 1) * blk, :] = jnp.transpose(out_t).astype(o_ref.dtype)
            outs = []


def _attention(proj3, pos3, invf):
    bsz, seq, _ = proj3.shape
    npairs = ATT_HEADS // 2
    nblk = seq // MOBA_BLOCK
    qkv_spec = lambda off: pl.BlockSpec((1, seq, LANES), lambda b, p: (b, 0, off // LANES + p))
    return pl.pallas_call(
        _attn_kernel,
        grid=(bsz, npairs),
        in_specs=[pl.BlockSpec((1, seq, 1), lambda b, p: (b, 0, 0)),
                  pl.BlockSpec((1, LANES), lambda b, p: (0, 0)),
                  qkv_spec(Q_OFF), qkv_spec(K_OFF), qkv_spec(V_OFF)],
        out_specs=pl.BlockSpec((1, seq, LANES), lambda b, p: (b, 0, p)),
        out_shape=jax.ShapeDtypeStruct((bsz, seq, ATT_WIDTH), BF16),
        scratch_shapes=[pltpu.VMEM((seq, LANES), F32), pltpu.VMEM((seq, LANES), F32),
                        pltpu.VMEM((seq, LANES), F32),
                        pltpu.VMEM((seq, 2 * LANES), BF16),
                        pltpu.VMEM((nblk, 2, ATT_HEAD_DIM + 2 * SUBLANES, MOBA_BLOCK), BF16),
                        pltpu.VMEM((2, nblk, 2 * LANES, MOBA_BLOCK), BF16),
                        pltpu.VMEM((2, nblk, MOBA_BLOCK, MOBA_BLOCK), F32)],
        compiler_params=_params(("parallel", "arbitrary")),
        name="moba_attention",
    )(pos3, invf, proj3, proj3, proj3)


def _split_pack(v, lane):
    hi = v.astype(BF16).astype(F32)
    lo = v - hi
    packed = jnp.where(lane < SSM_HEADS, hi, pltpu.roll(lo, SSM_HEADS, 1))
    return jnp.where(lane < 2 * SSM_HEADS, packed, 0.0).astype(BF16)


def _ssd_kernel(xbc_ref, z_ref, dt_ref, convw_ref, convb_ref, dtb_ref, alog_ref, dskip_ref,
                normw_ref, expand_ref, shift_ref, tri_ref, y_ref, tail_s, act_s, state_s):
    q = SSM_CHUNK
    lane = lax.broadcasted_iota(jnp.int32, (1, LANES), 1)

    @pl.when(pl.program_id(1) == 0)
    def _():
        tail_s[...] = jnp.zeros(tail_s.shape, F32)
        state_s[...] = jnp.zeros(state_s.shape, F32)

    strip = 4 * LANES
    sub = lax.broadcasted_iota(jnp.int32, (SUBLANES, 1), 0)
    for c0 in range(0, XBC_DIM, strip):
        cols = slice(c0, c0 + strip)
        u = xbc_ref[:, cols]
        back = _dot(shift_ref[...], u)
        conv = convb_ref[:, cols] + convw_ref[SSM_CONV - 1:SSM_CONV, cols] * u.astype(F32)
        head = conv[0:SUBLANES, :]
        tail = tail_s[:, cols]
        for d in range(1, SSM_CONV):
            w_d = convw_ref[SSM_CONV - 1 - d:SSM_CONV - d, cols]
            conv = conv + w_d * back[(d - 1) * q:d * q, :]
            head = head + w_d * (back[(d - 1) * q:(d - 1) * q + SUBLANES, :]
                                 + jnp.where(sub < d, pltpu.roll(tail, d, 0), 0.0))
        act_s[:, cols] = conv * _sigmoid(conv)
        act_s[0:SUBLANES, cols] = head * _sigmoid(head)
    tail_s[...] = xbc_ref[q - SUBLANES:q, :].astype(F32)

    dt_in = dt_ref[...] + dtb_ref[...]
    dt = jnp.maximum(dt_in, 0.0) + jnp.log1p(jnp.exp(-jnp.abs(dt_in)))
    a_dt = dt * (-jnp.exp(alog_ref[...]))
    row = lax.broadcasted_iota(jnp.int32, (q, q), 0)
    col = lax.broadcasted_iota(jnp.int32, (q, q), 1)
    causal = col <= row
    tri = tri_ref[...]
    p1 = a_dt.astype(BF16)
    r1 = a_dt - p1.astype(F32)
    p2 = r1.astype(BF16)
    p3 = (r1 - p2.astype(F32)).astype(BF16)
    acs = (_dot(tri, p1) + _dot(tri, p2) + _dot(tri, p3)) * LOG2E
    acs_t = jnp.transpose(acs)
    exp_a = jnp.exp2(acs)
    to_end = jnp.exp2(acs[q - 1:q, :] - acs)

    expand = expand_ref[...]
    dt_x = _dot(_split_pack(dt, lane), expand)
    exp_a_x = _dot(_split_pack(exp_a, lane), expand)
    to_end_x = _dot(_split_pack(to_end, lane), expand)

    pair_lane = lane // SSM_HEAD_DIM
    for g in range(SSM_GROUPS):
        cg = slice(g * GROUP_WIDTH, (g + 1) * GROUP_WIDTH)
        xs = act_s[:, cg]
        xdt = xs * dt_x[:, cg]
        b_off = D_INNER + g * SSM_STATE
        c_off = D_INNER + SSM_GROUPS * SSM_STATE + g * SSM_STATE
        bmat = act_s[:, b_off:b_off + SSM_STATE]
        cmat = act_s[:, c_off:c_off + SSM_STATE].astype(BF16)
        cb = jnp.where(causal, _dot_nt(cmat, bmat.astype(BF16)), 0.0)
        prev = state_s[g]
        y = _dot(cmat, prev.astype(BF16)) * exp_a_x[:, cg]
        y = y + xs * dskip_ref[:, cg]
        xdt_b = xdt.astype(BF16)
        diag = []
        for pr in range(HEADS_PER_GROUP // 2):
            xp = xdt_b[:, pr * LANES:(pr + 1) * LANES]
            outs = []
            for e in range(2):
                h = g * HEADS_PER_GROUP + 2 * pr + e
                seg = acs[:, h:h + 1] - acs_t[h:h + 1, :]
                decay = jnp.exp2(jnp.where(causal, seg, 0.0))
                outs.append(_dot((cb * decay).astype(BF16), xp))
            diag.append(jnp.where(pair_lane == 0, outs[0], outs[1]))
        y = y + jnp.concatenate(diag, axis=1)

        xw = (xdt * to_end_x[:, cg]).astype(BF16)
        new_state = _dot(jnp.transpose(bmat).astype(BF16), xw)
        state_s[g] = prev * exp_a_x[q - 1:q, cg] + new_state

        zg = z_ref[:, cg].astype(F32)
        hf = y * (zg * _sigmoid(zg))
        ms = jnp.mean(hf * hf, axis=1, keepdims=True)
        y_ref[:, cg] = (hf * lax.rsqrt(ms + RMS_EPS) * normw_ref[:, cg]).astype(y_ref.dtype)


def _ssd_constants():
    r = jnp.arange(LANES)[:, None]
    c = jnp.arange(D_INNER)[None, :]
    expand = ((r % SSM_HEADS) == (c // SSM_HEAD_DIM)) & (r < 2 * SSM_HEADS)
    t = jnp.arange((SSM_CONV - 1) * SSM_CHUNK)[:, None]
    s = jnp.arange(SSM_CHUNK)[None, :]
    shift = s == (t % SSM_CHUNK) - (t // SSM_CHUNK + 1)
    tri = jnp.arange(SSM_CHUNK)[None, :] <= jnp.arange(SSM_CHUNK)[:, None]
    return expand.astype(BF16), shift.astype(BF16), tri.astype(BF16)


def _ssd(proj2, dt2, bsz, seq, convw, convb, dtb, alog, dskip_x, normw, constants):
    expand, shift, tri = constants
    nc = seq // SSM_CHUNK
    q = SSM_CHUNK
    rowblk = lambda b, c: b * nc + c
    const = lambda shape: pl.BlockSpec(shape, lambda b, c: (0, 0))
    return pl.pallas_call(
        _ssd_kernel,
        grid=(bsz, nc),
        in_specs=[pl.BlockSpec((q, XBC_DIM), lambda b, c: (rowblk(b, c), XBC_OFF // XBC_DIM)),
                  pl.BlockSpec((q, D_INNER), lambda b, c: (rowblk(b, c), Z_OFF // D_INNER)),
                  pl.BlockSpec((q, LANES), lambda b, c: (rowblk(b, c), 0)),
                  const((SSM_CONV, XBC_DIM)), const((1, XBC_DIM)), const((1, LANES)),
                  const((1, LANES)), const((1, D_INNER)), const((1, D_INNER)),
                  const((LANES, D_INNER)), const(((SSM_CONV - 1) * q, q)), const((q, q))],
        out_specs=pl.BlockSpec((q, D_INNER), lambda b, c: (rowblk(b, c), 0)),
        out_shape=jax.ShapeDtypeStruct((bsz * seq, D_INNER), BF16),
        scratch_shapes=[pltpu.VMEM((SUBLANES, XBC_DIM), F32),
                        pltpu.VMEM((q, XBC_DIM), F32),
                        pltpu.VMEM((SSM_GROUPS, SSM_STATE, GROUP_WIDTH), F32)],
        compiler_params=_params(("parallel", "arbitrary")),
        name="ssd_scan",
    )(proj2, proj2, dt2, convw, convb, dtb, alog, dskip_x, normw, expand, shift, tri)


def _layer_norm(r, g, b):
    mu = jnp.mean(r, axis=1, keepdims=True)
    d = r - mu
    var = jnp.mean(d * d, axis=1, keepdims=True)
    return d * lax.rsqrt(var + LN_EPS) * g + b


def _merge_kernel(att_ref, y_ref, ga_ref, gs_ref, x_ref, wa_ref, ws_ref, wo_ref, g_ref, b_ref, o_ref):
    pa = _dot(att_ref[...], wa_ref[...])
    ps = _dot(y_ref[...], ws_ref[...])
    merged = _sigmoid(ga_ref[...].astype(F32)) * pa + _sigmoid(gs_ref[...].astype(F32)) * ps
    mix = _dot(merged.astype(BF16), wo_ref[...])
    o_ref[...] = _layer_norm(ALPHA * x_ref[...] + mix, g_ref[...], b_ref[...])


def _resident(shape):
    return pl.BlockSpec(shape, lambda i: (0, 0), pipeline_mode=pl.Buffered(1))


def _merge(att2, y2, proj2, x2, wa, ws, wo, g, b, tm):
    m = x2.shape[0]
    return pl.pallas_call(
        _merge_kernel,
        grid=(m // tm,),
        in_specs=[pl.BlockSpec((tm, ATT_WIDTH), lambda i: (i, 0)),
                  pl.BlockSpec((tm, D_INNER), lambda i: (i, 0)),
                  pl.BlockSpec((tm, D_MODEL), lambda i: (i, GA_OFF // D_MODEL)),
                  pl.BlockSpec((tm, D_MODEL), lambda i: (i, GS_OFF // D_MODEL)),
                  pl.BlockSpec((tm, D_MODEL), lambda i: (i, 0)),
                  _resident((ATT_WIDTH, D_MODEL)), _resident((D_INNER, D_MODEL)),
                  _resident((D_MODEL, D_MODEL)), _resident((1, D_MODEL)), _resident((1, D_MODEL))],
        out_specs=pl.BlockSpec((tm, D_MODEL), lambda i: (i, 0)),
        out_shape=jax.ShapeDtypeStruct((m, D_MODEL), F32),
        compiler_params=_params(("parallel",)),
        name="merge_ln",
    )(att2, y2, proj2, proj2, x2, wa, ws, wo, g, b)


def _mlp_kernel(h_ref, wu_ref, wd_ref, g_ref, b_ref, o_ref):
    h = h_ref[...]
    up = _dot(h.astype(BF16), wu_ref[...])
    act = jnp.square(jnp.maximum(up, 0.0)).astype(BF16)
    down = _dot(act, wd_ref[...])
    o_ref[...] = _layer_norm(ALPHA * h + down, g_ref[...], b_ref[...])


def _mlp(h2, wu, wd, g, b, tm):
    m = h2.shape[0]
    return pl.pallas_call(
        _mlp_kernel,
        grid=(m // tm,),
        in_specs=[pl.BlockSpec((tm, D_MODEL), lambda i: (i, 0)),
                  _resident((D_MODEL, D_FF)), _resident((D_FF, D_MODEL)),
                  _resident((1, D_MODEL)), _resident((1, D_MODEL))],
        out_specs=pl.BlockSpec((tm, D_MODEL), lambda i: (i, 0)),
        out_shape=jax.ShapeDtypeStruct((m, D_MODEL), F32),
        compiler_params=_params(("parallel",)),
        name="mlp_ln",
    )(h2, wu, wd, g, b)


def _pad_lanes(v):
    return jnp.pad(v.astype(F32), (0, LANES - v.shape[0])).reshape(1, LANES)


def _rope_lane_freqs():
    inv_freq = ROPE_THETA ** (-jnp.arange(0, ROT_DIM, 2, dtype=F32) / ROT_DIM)
    per_head = jnp.concatenate([inv_freq, inv_freq, jnp.zeros((ATT_HEAD_DIM - ROT_DIM,), F32)])
    return jnp.tile(per_head, LANES // ATT_HEAD_DIM).reshape(1, LANES)


def kernel(x, positions, ln1_g, ln1_b, w_in, conv_w, conv_b, dt_bias, a_log, d_skip, ssm_norm_w,
           w_attn_proj, w_ssm_proj, w_out, ln2_g, ln2_b, w_up, w_down):
    bsz, seq, d_model = x.shape
    assert d_model == D_MODEL and seq % MOBA_BLOCK == 0 and seq % SSM_CHUNK == 0
    assert w_in.shape == (DEPTH, D_MODEL, sum(IN_SPLITS))
    rows = bsz * seq
    tm_proj = 1024 if rows % 1024 == 0 else SSM_CHUNK
    tm_row = 512 if rows % 512 == 0 else SSM_CHUNK

    pos3 = positions.reshape(bsz, seq, 1)
    invf = _rope_lane_freqs()
    ssd_constants = _ssd_constants()

    x2 = x.reshape(rows, D_MODEL)
    for i in range(DEPTH):
        wq, wk, wv, wz, wxbc, wdt, wga, wgs = jnp.split(
            w_in[i], [int(sum(IN_SPLITS[:n])) for n in range(1, len(IN_SPLITS))], axis=1)
        w_cat = jnp.concatenate([wxbc, wz, wq, wk, wv, wga, wgs], axis=1).astype(BF16)
        w_dt = jnp.pad(wdt, ((0, 0), (0, LANES - SSM_HEADS))).astype(BF16)

        proj, dt_raw = _projection(x2, w_cat, w_dt, tm_proj, PROJ_WIDTH // 4)

        att = _attention(proj.reshape(bsz, seq, PROJ_WIDTH), pos3, invf)
        y_ssm = _ssd(proj, dt_raw, bsz, seq, conv_w[i], conv_b[i].reshape(1, XBC_DIM),
                     _pad_lanes(dt_bias[i]), _pad_lanes(a_log[i]),
                     jnp.repeat(d_skip[i].astype(F32), SSM_HEAD_DIM).reshape(1, D_INNER),
                     ssm_norm_w[i].reshape(1, D_INNER), ssd_constants)
        h = _merge(att.reshape(rows, ATT_WIDTH), y_ssm, proj, x2,
                   w_attn_proj[i].astype(BF16), w_ssm_proj[i].astype(BF16), w_out[i].astype(BF16),
                   ln1_g[i].reshape(1, D_MODEL), ln1_b[i].reshape(1, D_MODEL), tm_row)
        x2 = _mlp(h, w_up[i].astype(BF16), w_down[i].astype(BF16),
                  ln2_g[i].reshape(1, D_MODEL), ln2_b[i].reshape(1, D_MODEL), tm_row)
    return x2.reshape(bsz, seq, D_MODEL)
```

```python
import functools

import jax
import jax.numpy as jnp
from jax import lax
from jax.experimental import pallas as pl
from jax.experimental.pallas import tpu as pltpu

F32 = jnp.float32
BF16 = jnp.bfloat16

D_MODEL = 1024
DEPTH = 2
ATT_HEAD_DIM = 64
ATT_HEADS = D_MODEL // ATT_HEAD_DIM
ATT_WIDTH = ATT_HEADS * ATT_HEAD_DIM
ROT_DIM = ATT_HEAD_DIM // 4
ROT_HALF = ROT_DIM // 2
ROPE_THETA = 500000.0
MOBA_BLOCK = 256
MOBA_TOPK = 3
D_INNER = 2 * D_MODEL
SSM_HEAD_DIM = 64
SSM_HEADS = D_INNER // SSM_HEAD_DIM
SSM_GROUPS = 8
SSM_STATE = 128
SSM_CONV = 4
SSM_CHUNK = 256
XBC_DIM = D_INNER + 2 * SSM_GROUPS * SSM_STATE
D_FF = 4 * D_MODEL
ALPHA = (2 * DEPTH) ** 0.25
LN_EPS = 1e-5
RMS_EPS = 1e-5
IN_SPLITS = (ATT_WIDTH, ATT_WIDTH, ATT_WIDTH, D_INNER, XBC_DIM, SSM_HEADS, D_MODEL, D_MODEL)

LANES = 128
MXU_WIDTH = 256
SUBLANES = 8
VMEM_LIMIT = 56 * 1024 * 1024

XBC_OFF = 0
Z_OFF = XBC_OFF + XBC_DIM
Q_OFF = Z_OFF + D_INNER
K_OFF = Q_OFF + ATT_WIDTH
V_OFF = K_OFF + ATT_WIDTH
GA_OFF = V_OFF + ATT_WIDTH
GS_OFF = GA_OFF + D_MODEL
PROJ_WIDTH = GS_OFF + D_MODEL

HEADS_PER_GROUP = SSM_HEADS // SSM_GROUPS
GROUP_WIDTH = HEADS_PER_GROUP * SSM_HEAD_DIM
NEG_INF = float("-inf")
MASK_BIAS = -1e30
LOG2E = 1.4426950408889634
Q_SCALE = ATT_HEAD_DIM ** -0.5 * LOG2E


def _params(semantics):
    return pltpu.CompilerParams(dimension_semantics=semantics, vmem_limit_bytes=VMEM_LIMIT)


def _dot(a, b):
    return jnp.dot(a, b, preferred_element_type=F32)


def _dot_nt(a, b, precision=None):
    return lax.dot_general(a, b, (((1,), (1,)), ((), ())), precision=precision,
                           preferred_element_type=F32)


def _sigmoid(x):
    return 1.0 / (1.0 + jnp.exp2(x * (-LOG2E)))


def _proj_kernel(pos_ref, invf_ref, x_ref, w_ref, wdt_ref, o_ref, dt_ref, xb_ref, cos_s, sa_s, sb_s,
                 *, rope_block, rope_lo, rope_mid, rope_hi):
    j = pl.program_id(1)
    tn = o_ref.shape[1]

    @pl.when(j == 0)
    def _():
        xb_ref[...] = x_ref[...].astype(BF16)
        dt_ref[...] = _dot(xb_ref[...], wdt_ref[...])
        ang = invf_ref[...] * pos_ref[0].astype(F32)
        c8 = jnp.cos(ang)
        s8 = jnp.sin(ang)
        one8 = jnp.ones_like(c8)
        zero8 = jnp.zeros_like(c8)
        idle = ATT_HEAD_DIM // ROT_HALF - 2
        per_head = ((cos_s, [c8, c8] + [one8] * idle),
                    (sa_s, [-s8, zero8] + [zero8] * idle),
                    (sb_s, [zero8, s8] + [zero8] * idle))
        for table, groups in per_head:
            t = jnp.transpose(jnp.concatenate(groups * (MXU_WIDTH // ATT_HEAD_DIM), axis=0))
            table[...] = t

    def strips(with_rope):
        for c0 in range(0, tn, MXU_WIDTH):
            cols = slice(c0, min(c0 + MXU_WIDTH, tn))
            acc = _dot(xb_ref[...], w_ref[:, cols])
            if with_rope and rope_lo <= c0 < rope_hi:
                acc = (acc * cos_s[...]
                       + pltpu.roll(acc, MXU_WIDTH - ROT_HALF, 1) * sa_s[...]
                       + pltpu.roll(acc, ROT_HALF, 1) * sb_s[...])
                if c0 < rope_mid:
                    acc = acc * Q_SCALE
            o_ref[:, cols] = acc.astype(o_ref.dtype)

    @pl.when(j == rope_block)
    def _():
        strips(True)

    @pl.when(j != rope_block)
    def _():
        strips(False)


def _projection(x, pos, invf, w, w_dt, tm, tn):
    m, k = x.shape
    n = w.shape[1]
    rope_block = Q_OFF // tn
    assert (V_OFF - 1) // tn == rope_block and (Q_OFF - rope_block * tn) % MXU_WIDTH == 0
    kern = functools.partial(_proj_kernel, rope_block=rope_block, rope_lo=Q_OFF - rope_block * tn,
                             rope_mid=K_OFF - rope_block * tn, rope_hi=V_OFF - rope_block * tn)
    return pl.pallas_call(
        kern,
        grid=(m // tm, n // tn),
        in_specs=[pl.BlockSpec((1, 1, tm), lambda i, j: (i, 0, 0)),
                  pl.BlockSpec((SUBLANES, 1), lambda i, j: (0, 0)),
                  pl.BlockSpec((tm, k), lambda i, j: (i, 0)),
                  pl.BlockSpec((k, tn), lambda i, j: (0, j)),
                  pl.BlockSpec((k, LANES), lambda i, j: (0, 0))],
        out_specs=[pl.BlockSpec((tm, tn), lambda i, j: (i, j)),
                   pl.BlockSpec((tm, LANES), lambda i, j: (i, 0))],
        out_shape=[jax.ShapeDtypeStruct((m, n), BF16), jax.ShapeDtypeStruct((m, LANES), F32)],
        scratch_shapes=[pltpu.VMEM((tm, k), BF16), pltpu.VMEM((tm, MXU_WIDTH), F32),
                        pltpu.VMEM((tm, MXU_WIDTH), F32), pltpu.VMEM((tm, MXU_WIDTH), F32)],
        compiler_params=_params(("parallel", "arbitrary")),
        name="proj_matmul",
    )(pos.reshape(m // tm, 1, tm), invf, x, w, w_dt)


def _attn_kernel(q_ref, k_ref, v_ref, o_ref, k_s, vt_s, rhs_s, s_s):
    seq = q_ref.shape[1]
    nblk = seq // MOBA_BLOCK
    blk = MOBA_BLOCK
    lane = lax.broadcasted_iota(jnp.int32, (1, LANES), 1)
    head_of_lane = lane // ATT_HEAD_DIM

    @pl.when(pl.program_id(1) == 0)
    def _():
        blk_of_row = lax.broadcasted_iota(jnp.int32, (seq, LANES), 0) // blk
        k_s[:, LANES:] = jnp.where(blk_of_row == lane, 1.0, 0.0).astype(BF16)

    hd = ATT_HEAD_DIM
    extra = vt_s.shape[2] - hd
    ones_row = jnp.where(lax.broadcasted_iota(jnp.int32, (extra, blk), 0) == 0, 1.0, 0.0).astype(BF16)
    kmeans = []
    for j in range(nblk):
        rows = pl.ds(j * blk, blk)
        kj = k_ref[0, rows, :]
        k_s[rows, 0:LANES] = kj
        kmeans.append(jnp.sum(kj.astype(F32), axis=0, keepdims=True) * (1.0 / blk))
        vt = jnp.transpose(v_ref[0, rows, :].astype(F32)).astype(BF16)
        for h in range(2):
            vt_s[j, h, 0:hd, :] = vt[h * hd:(h + 1) * hd, :]
            vt_s[j, h, hd:, :] = ones_row
    kmean = jnp.concatenate(kmeans, axis=0)

    row_n = lax.broadcasted_iota(jnp.int32, (nblk, blk), 0)
    bias_rows = 2 * SUBLANES
    for qb in range(nblk):
        rows = pl.ds(qb * blk, blk)
        qt = jnp.transpose(q_ref[0, rows, :].astype(F32))
        for h in range(2):
            past = jnp.where(row_n < qb, 1.0, 0.0)
            if qb <= MOBA_TOPK:
                sel = past
            else:
                kmh = jnp.where(head_of_lane == h, kmean, 0.0)
                gate = jnp.dot(kmh, qt, precision=lax.Precision.HIGHEST,
                               preferred_element_type=F32)
                cnt = jnp.zeros((nblk, blk), F32)
                for m in range(qb):
                    gm = gate[m:m + 1, :]
                    ge = jnp.where(gm >= gate, 1.0, 0.0)
                    gt = jnp.where(gm > gate, 1.0, 0.0)
                    cnt = cnt + jnp.where(row_n > m, ge, jnp.where(row_n < m, gt, 0.0))
                sel = jnp.where(cnt < MOBA_TOPK, past, 0.0)
            attended = jnp.where(row_n == qb, 1.0, sel)
            bias = jnp.where(attended > 0.5, 0.0, MASK_BIAS)
            rhs_s[h, qb, h * hd:(h + 1) * hd, :] = qt[h * hd:(h + 1) * hd, :].astype(BF16)
            rhs_s[h, qb, (1 - h) * hd:(2 - h) * hd, :] = jnp.zeros((hd, blk), BF16)
            rhs_s[h, qb, LANES:LANES + bias_rows, :] = jnp.concatenate(
                [bias, jnp.zeros((bias_rows - nblk, blk), F32)], axis=0).astype(BF16)
            rhs_s[h, qb, LANES + bias_rows:, :] = jnp.zeros((LANES - bias_rows, blk), BF16)

    key_row = lax.broadcasted_iota(jnp.int32, (blk, blk), 0)
    query_col = lax.broadcasted_iota(jnp.int32, (blk, blk), 1)
    causal = key_row <= query_col

    def score_tile(qb, h, j):
        s = _dot(k_s[j * blk:(j + 1) * blk, :], rhs_s[h, qb])
        if j == qb:
            s = jnp.where(causal, s, NEG_INF)
        s_s[h, j] = s
        return jnp.max(s, axis=0, keepdims=True)

    def value_tile(h, j, m):
        p = jnp.exp2(s_s[h, j] - m)
        return _dot(vt_s[j, h], p.astype(BF16))

    def fold_max(m, mj):
        return mj if m is None else jnp.maximum(m, mj)

    units = [(qb, h) for qb in range(nblk) for h in range(2)]
    m_next = None
    for j in range(units[0][0] + 1):
        m_next = fold_max(m_next, score_tile(*units[0], j))
    outs = []
    for i, (qb, h) in enumerate(units):
        m, m_next = m_next, None
        nxt = units[i + 1] if i + 1 < len(units) else None
        todo_next = list(range(nxt[0] + 1)) if nxt else []
        acc = None
        for j in range(qb + 1):
            if todo_next:
                m_next = fold_max(m_next, score_tile(*nxt, todo_next.pop(0)))
            pv = value_tile(h, j, m)
            acc = pv if acc is None else acc + pv
        while todo_next:
            m_next = fold_max(m_next, score_tile(*nxt, todo_next.pop(0)))
        outs.append(acc[0:hd, :] / acc[hd:hd + 1, :])
        if h == 1:
            out_t = jnp.concatenate(outs, axis=0)
            o_ref[0, qb * blk:(qb + 1) * blk, :] = jnp.transpose(out_t).astype(o_ref.dtype)
            outs = []


def _attention(proj3):
    bsz, seq, _ = proj3.shape
    npairs = ATT_HEADS // 2
    nblk = seq // MOBA_BLOCK
    qkv_spec = lambda off: pl.BlockSpec((1, seq, LANES), lambda b, p: (b, 0, off // LANES + p))
    return pl.pallas_call(
        _attn_kernel,
        grid=(bsz, npairs),
        in_specs=[qkv_spec(Q_OFF), qkv_spec(K_OFF), qkv_spec(V_OFF)],
        out_specs=pl.BlockSpec((1, seq, LANES), lambda b, p: (b, 0, p)),
        out_shape=jax.ShapeDtypeStruct((bsz, seq, ATT_WIDTH), BF16),
        scratch_shapes=[pltpu.VMEM((seq, 2 * LANES), BF16),
                        pltpu.VMEM((nblk, 2, ATT_HEAD_DIM + 2 * SUBLANES, MOBA_BLOCK), BF16),
                        pltpu.VMEM((2, nblk, 2 * LANES, MOBA_BLOCK), BF16),
                        pltpu.VMEM((2, nblk, MOBA_BLOCK, MOBA_BLOCK), F32)],
        compiler_params=_params(("parallel", "arbitrary")),
        name="moba_attention",
    )(proj3, proj3, proj3)


def _split_pack(v, lane):
    hi = v.astype(BF16).astype(F32)
    lo = v - hi
    packed = jnp.where(lane < SSM_HEADS, hi, pltpu.roll(lo, SSM_HEADS, 1))
    return jnp.where(lane < 2 * SSM_HEADS, packed, 0.0).astype(BF16)


def _ssd_kernel(xbc_ref, z_ref, dt_ref, convw_ref, convb_ref, dtb_ref, alog_ref, dskip_ref,
                normw_ref, expand_ref, shift_ref, tri_ref, y_ref, tail_s, act_s, state_s):
    q = SSM_CHUNK
    lane = lax.broadcasted_iota(jnp.int32, (1, LANES), 1)

    @pl.when(pl.program_id(1) == 0)
    def _():
        tail_s[...] = jnp.zeros(tail_s.shape, F32)
        state_s[...] = jnp.zeros(state_s.shape, F32)

    strip = 4 * LANES
    sub = lax.broadcasted_iota(jnp.int32, (SUBLANES, 1), 0)
    for c0 in range(0, XBC_DIM, strip):
        cols = slice(c0, c0 + strip)
        u = xbc_ref[:, cols]
        back = _dot(shift_ref[...], u)
        conv = convb_ref[:, cols] + convw_ref[SSM_CONV - 1:SSM_CONV, cols] * u.astype(F32)
        head = conv[0:SUBLANES, :]
        tail = tail_s[:, cols]
        for d in range(1, SSM_CONV):
            w_d = convw_ref[SSM_CONV - 1 - d:SSM_CONV - d, cols]
            conv = conv + w_d * back[(d - 1) * q:d * q, :]
            head = head + w_d * (back[(d - 1) * q:(d - 1) * q + SUBLANES, :]
                                 + jnp.where(sub < d, pltpu.roll(tail, d, 0), 0.0))
        act_s[:, cols] = conv * _sigmoid(conv)
        act_s[0:SUBLANES, cols] = head * _sigmoid(head)
    tail_s[...] = xbc_ref[q - SUBLANES:q, :].astype(F32)

    dt_in = dt_ref[...] + dtb_ref[...]
    dt = jnp.maximum(dt_in, 0.0) + jnp.log1p(jnp.exp(-jnp.abs(dt_in)))
    a_dt = dt * (-jnp.exp(alog_ref[...]))
    row = lax.broadcasted_iota(jnp.int32, (q, q), 0)
    col = lax.broadcasted_iota(jnp.int32, (q, q), 1)
    causal = col <= row
    tri = tri_ref[...]
    p1 = a_dt.astype(BF16)
    r1 = a_dt - p1.astype(F32)
    p2 = r1.astype(BF16)
    p3 = (r1 - p2.astype(F32)).astype(BF16)
    acs = (_dot(tri, p1) + _dot(tri, p2) + _dot(tri, p3)) * LOG2E
    acs_t = jnp.transpose(acs)
    exp_a = jnp.exp2(acs)
    to_end = jnp.exp2(acs[q - 1:q, :] - acs)

    expand = expand_ref[...]
    dt_x = _dot(_split_pack(dt, lane), expand)
    exp_a_x = _dot(_split_pack(exp_a, lane), expand)
    to_end_x = _dot(_split_pack(to_end, lane), expand)

    pair_lane = lane // SSM_HEAD_DIM
    for g in range(SSM_GROUPS):
        cg = slice(g * GROUP_WIDTH, (g + 1) * GROUP_WIDTH)
        xs = act_s[:, cg]
        xdt = xs * dt_x[:, cg]
        b_off = D_INNER + g * SSM_STATE
        c_off = D_INNER + SSM_GROUPS * SSM_STATE + g * SSM_STATE
        bmat = act_s[:, b_off:b_off + SSM_STATE]
        cmat = act_s[:, c_off:c_off + SSM_STATE].astype(BF16)
        cb = jnp.where(causal, _dot_nt(cmat, bmat.astype(BF16)), 0.0)
        prev = state_s[g]
        y = _dot(cmat, prev.astype(BF16)) * exp_a_x[:, cg]
        y = y + xs * dskip_ref[:, cg]
        xdt_b = xdt.astype(BF16)
        diag = []
        for pr in range(HEADS_PER_GROUP // 2):
            xp = xdt_b[:, pr * LANES:(pr + 1) * LANES]
            outs = []
            for e in range(2):
                h = g * HEADS_PER_GROUP + 2 * pr + e
                seg = acs[:, h:h + 1] - acs_t[h:h + 1, :]
                decay = jnp.exp2(jnp.where(causal, seg, 0.0))
                outs.append(_dot((cb * decay).astype(BF16), xp))
            diag.append(jnp.where(pair_lane == 0, outs[0], outs[1]))
        y = y + jnp.concatenate(diag, axis=1)

        xw = (xdt * to_end_x[:, cg]).astype(BF16)
        new_state = _dot(jnp.transpose(bmat).astype(BF16), xw)
        state_s[g] = prev * exp_a_x[q - 1:q, cg] + new_state

        zg = z_ref[:, cg].astype(F32)
        hf = y * (zg * _sigmoid(zg))
        ms = jnp.mean(hf * hf, axis=1, keepdims=True)
        y_ref[:, cg] = (hf * lax.rsqrt(ms + RMS_EPS) * normw_ref[:, cg]).astype(y_ref.dtype)


def _ssd_constants():
    r = jnp.arange(LANES)[:, None]
    c = jnp.arange(D_INNER)[None, :]
    expand = ((r % SSM_HEADS) == (c // SSM_HEAD_DIM)) & (r < 2 * SSM_HEADS)
    t = jnp.arange((SSM_CONV - 1) * SSM_CHUNK)[:, None]
    s = jnp.arange(SSM_CHUNK)[None, :]
    shift = s == (t % SSM_CHUNK) - (t // SSM_CHUNK + 1)
    tri = jnp.arange(SSM_CHUNK)[None, :] <= jnp.arange(SSM_CHUNK)[:, None]
    return expand.astype(BF16), shift.astype(BF16), tri.astype(BF16)


def _ssd(proj2, dt2, bsz, seq, convw, convb, dtb, alog, dskip_x, normw, constants):
    expand, shift, tri = constants
    nc = seq // SSM_CHUNK
    q = SSM_CHUNK
    rowblk = lambda b, c: b * nc + c
    const = lambda shape: pl.BlockSpec(shape, lambda b, c: (0, 0))
    return pl.pallas_call(
        _ssd_kernel,
        grid=(bsz, nc),
        in_specs=[pl.BlockSpec((q, XBC_DIM), lambda b, c: (rowblk(b, c), XBC_OFF // XBC_DIM)),
                  pl.BlockSpec((q, D_INNER), lambda b, c: (rowblk(b, c), Z_OFF // D_INNER)),
                  pl.BlockSpec((q, LANES), lambda b, c: (rowblk(b, c), 0)),
                  const((SSM_CONV, XBC_DIM)), const((1, XBC_DIM)), const((1, LANES)),
                  const((1, LANES)), const((1, D_INNER)), const((1, D_INNER)),
                  const((LANES, D_INNER)), const(((SSM_CONV - 1) * q, q)), const((q, q))],
        out_specs=pl.BlockSpec((q, D_INNER), lambda b, c: (rowblk(b, c), 0)),
        out_shape=jax.ShapeDtypeStruct((bsz * seq, D_INNER), BF16),
        scratch_shapes=[pltpu.VMEM((SUBLANES, XBC_DIM), F32),
                        pltpu.VMEM((q, XBC_DIM), F32),
                        pltpu.VMEM((SSM_GROUPS, SSM_STATE, GROUP_WIDTH), F32)],
        compiler_params=_params(("parallel", "arbitrary")),
        name="ssd_scan",
    )(proj2, proj2, dt2, convw, convb, dtb, alog, dskip_x, normw, expand, shift, tri)


def _layer_norm(r, g, b):
    mu = jnp.mean(r, axis=1, keepdims=True)
    d = r - mu
    var = jnp.mean(d * d, axis=1, keepdims=True)
    return d * lax.rsqrt(var + LN_EPS) * g + b


def _merge_kernel(att_ref, y_ref, ga_ref, gs_ref, x_ref, wa_ref, ws_ref, wo_ref, g_ref, b_ref, o_ref):
    pa = _dot(att_ref[...], wa_ref[...])
    ps = _dot(y_ref[...], ws_ref[...])
    merged = _sigmoid(ga_ref[...].astype(F32)) * pa + _sigmoid(gs_ref[...].astype(F32)) * ps
    mix = _dot(merged.astype(BF16), wo_ref[...])
    o_ref[...] = _layer_norm(ALPHA * x_ref[...] + mix, g_ref[...], b_ref[...])


def _resident(shape):
    return pl.BlockSpec(shape, lambda i: (0, 0), pipeline_mode=pl.Buffered(1))


def _merge(att2, y2, proj2, x2, wa, ws, wo, g, b, tm):
    m = x2.shape[0]
    return pl.pallas_call(
        _merge_kernel,
        grid=(m // tm,),
        in_specs=[pl.BlockSpec((tm, ATT_WIDTH), lambda i: (i, 0)),
                  pl.BlockSpec((tm, D_INNER), lambda i: (i, 0)),
                  pl.BlockSpec((tm, D_MODEL), lambda i: (i, GA_OFF // D_MODEL)),
                  pl.BlockSpec((tm, D_MODEL), lambda i: (i, GS_OFF // D_MODEL)),
                  pl.BlockSpec((tm, D_MODEL), lambda i: (i, 0)),
                  _resident((ATT_WIDTH, D_MODEL)), _resident((D_INNER, D_MODEL)),
                  _resident((D_MODEL, D_MODEL)), _resident((1, D_MODEL)), _resident((1, D_MODEL))],
        out_specs=pl.BlockSpec((tm, D_MODEL), lambda i: (i, 0)),
        out_shape=jax.ShapeDtypeStruct((m, D_MODEL), F32),
        compiler_params=_params(("parallel",)),
        name="merge_ln",
    )(att2, y2, proj2, proj2, x2, wa, ws, wo, g, b)


def _mlp_kernel(h_ref, wu_ref, wd_ref, g_ref, b_ref, o_ref):
    h = h_ref[...]
    up = _dot(h.astype(BF16), wu_ref[...])
    act = jnp.square(jnp.maximum(up, 0.0)).astype(BF16)
    down = _dot(act, wd_ref[...])
    o_ref[...] = _layer_norm(ALPHA * h + down, g_ref[...], b_ref[...])


def _mlp(h2, wu, wd, g, b, tm):
    m = h2.shape[0]
    return pl.pallas_call(
        _mlp_kernel,
        grid=(m // tm,),
        in_specs=[pl.BlockSpec((tm, D_MODEL), lambda i: (i, 0)),
                  _resident((D_MODEL, D_FF)), _resident((D_FF, D_MODEL)),
                  _resident((1, D_MODEL)), _resident((1, D_MODEL))],
        out_specs=pl.BlockSpec((tm, D_MODEL), lambda i: (i, 0)),
        out_shape=jax.ShapeDtypeStruct((m, D_MODEL), F32),
        compiler_params=_params(("parallel",)),
        name="mlp_ln",
    )(h2, wu, wd, g, b)


def _pad_lanes(v):
    return jnp.pad(v.astype(F32), (0, LANES - v.shape[0])).reshape(1, LANES)


def _rope_freqs():
    inv_freq = ROPE_THETA ** (-jnp.arange(0, ROT_DIM, 2, dtype=F32) / ROT_DIM)
    return inv_freq.reshape(ROT_HALF, 1)


def kernel(x, positions, ln1_g, ln1_b, w_in, conv_w, conv_b, dt_bias, a_log, d_skip, ssm_norm_w,
           w_attn_proj, w_ssm_proj, w_out, ln2_g, ln2_b, w_up, w_down):
    bsz, seq, d_model = x.shape
    assert d_model == D_MODEL and seq % MOBA_BLOCK == 0 and seq % SSM_CHUNK == 0
    assert w_in.shape == (DEPTH, D_MODEL, sum(IN_SPLITS))
    rows = bsz * seq
    tm_proj = 1024 if rows % 1024 == 0 else SSM_CHUNK
    tm_row = 512 if rows % 512 == 0 else SSM_CHUNK

    invf = _rope_freqs()
    ssd_constants = _ssd_constants()

    ends = [int(sum(IN_SPLITS[:n])) for n in range(len(IN_SPLITS) + 1)]
    wq, wk, wv, wz, wxbc, wdt, wga, wgs = [w_in[:, :, a:b] for a, b in zip(ends[:-1], ends[1:])]
    w_cat = jnp.concatenate([wxbc, wz, wq, wk, wv, wga, wgs], axis=2).astype(BF16)
    w_dt = jnp.pad(wdt, ((0, 0), (0, 0), (0, LANES - SSM_HEADS))).astype(BF16)

    x2 = x.reshape(rows, D_MODEL)
    for i in range(DEPTH):
        proj, dt_raw = _projection(x2, positions, invf, w_cat[i], w_dt[i], tm_proj, PROJ_WIDTH // 4)

        att = _attention(proj.reshape(bsz, seq, PROJ_WIDTH))
        y_ssm = _ssd(proj, dt_raw, bsz, seq, conv_w[i], conv_b[i].reshape(1, XBC_DIM),
                     _pad_lanes(dt_bias[i]), _pad_lanes(a_log[i]),
                     jnp.repeat(d_skip[i].astype(F32), SSM_HEAD_DIM).reshape(1, D_INNER),
                     ssm_norm_w[i].reshape(1, D_INNER), ssd_constants)
        h = _merge(att.reshape(rows, ATT_WIDTH), y_ssm, proj, x2,
                   w_attn_proj[i].astype(BF16), w_ssm_proj[i].astype(BF16), w_out[i].astype(BF16),
                   ln1_g[i].reshape(1, D_MODEL), ln1_b[i].reshape(1, D_MODEL), tm_row)
        x2 = _mlp(h, w_up[i].astype(BF16), w_down[i].astype(BF16),
                  ln2_g[i].reshape(1, D_MODEL), ln2_b[i].reshape(1, D_MODEL), tm_row)
    return x2.reshape(bsz, seq, D_MODEL)
```

```python
import functools

import jax
import jax.numpy as jnp
from jax import lax
from jax.experimental import pallas as pl
from jax.experimental.pallas import tpu as pltpu

F32 = jnp.float32
BF16 = jnp.bfloat16

D_MODEL = 1024
DEPTH = 2
ATT_HEAD_DIM = 64
ATT_HEADS = D_MODEL // ATT_HEAD_DIM
ATT_WIDTH = ATT_HEADS * ATT_HEAD_DIM
ROT_DIM = ATT_HEAD_DIM // 4
ROT_HALF = ROT_DIM // 2
ROPE_THETA = 500000.0
MOBA_BLOCK = 256
MOBA_TOPK = 3
D_INNER = 2 * D_MODEL
SSM_HEAD_DIM = 64
SSM_HEADS = D_INNER // SSM_HEAD_DIM
SSM_GROUPS = 8
SSM_STATE = 128
SSM_CONV = 4
SSM_CHUNK = 256
XBC_DIM = D_INNER + 2 * SSM_GROUPS * SSM_STATE
D_FF = 4 * D_MODEL
ALPHA = (2 * DEPTH) ** 0.25
LN_EPS = 1e-5
RMS_EPS = 1e-5
IN_SPLITS = (ATT_WIDTH, ATT_WIDTH, ATT_WIDTH, D_INNER, XBC_DIM, SSM_HEADS, D_MODEL, D_MODEL)

LANES = 128
MXU_WIDTH = 256
SUBLANES = 8
VMEM_LIMIT = 56 * 1024 * 1024

XBC_OFF = 0
Z_OFF = XBC_OFF + XBC_DIM
Q_OFF = Z_OFF + D_INNER
K_OFF = Q_OFF + ATT_WIDTH
V_OFF = K_OFF + ATT_WIDTH
GA_OFF = V_OFF + ATT_WIDTH
GS_OFF = GA_OFF + D_MODEL
PROJ_WIDTH = GS_OFF + D_MODEL

HEADS_PER_GROUP = SSM_HEADS // SSM_GROUPS
GROUP_WIDTH = HEADS_PER_GROUP * SSM_HEAD_DIM
NEG_INF = float("-inf")
MASK_BIAS = -1e30
LOG2E = 1.4426950408889634
Q_SCALE = ATT_HEAD_DIM ** -0.5 * LOG2E


def _params(semantics):
    return pltpu.CompilerParams(dimension_semantics=semantics, vmem_limit_bytes=VMEM_LIMIT)


def _dot(a, b):
    return jnp.dot(a, b, preferred_element_type=F32)


def _dot_nt(a, b, precision=None):
    return lax.dot_general(a, b, (((1,), (1,)), ((), ())), precision=precision,
                           preferred_element_type=F32)


def _sigmoid(x):
    return 1.0 / (1.0 + jnp.exp2(x * (-LOG2E)))


def _proj_kernel(pos_ref, invf_ref, x_ref, w_ref, wdt_ref, o_ref, dt_ref, xb_ref, cos_s, sa_s, sb_s,
                 *, rope_block, rope_lo, rope_mid, rope_hi):
    j = pl.program_id(1)
    tn = o_ref.shape[1]

    @pl.when(j == 0)
    def _():
        xb_ref[...] = x_ref[...].astype(BF16)
        dt_ref[...] = _dot(xb_ref[...], wdt_ref[...])
        ang = invf_ref[...] * pos_ref[0].astype(F32)
        c8 = jnp.cos(ang)
        s8 = jnp.sin(ang)
        one8 = jnp.ones_like(c8)
        zero8 = jnp.zeros_like(c8)
        idle = ATT_HEAD_DIM // ROT_HALF - 2
        per_head = ((cos_s, [c8, c8] + [one8] * idle),
                    (sa_s, [-s8, zero8] + [zero8] * idle),
                    (sb_s, [zero8, s8] + [zero8] * idle))
        for table, groups in per_head:
            t = jnp.transpose(jnp.concatenate(groups * (MXU_WIDTH // ATT_HEAD_DIM), axis=0))
            table[...] = t

    def strips(with_rope):
        for c0 in range(0, tn, MXU_WIDTH):
            cols = slice(c0, min(c0 + MXU_WIDTH, tn))
            acc = _dot(xb_ref[...], w_ref[:, cols])
            if with_rope and rope_lo <= c0 < rope_hi:
                acc = (acc * cos_s[...]
                       + pltpu.roll(acc, MXU_WIDTH - ROT_HALF, 1) * sa_s[...]
                       + pltpu.roll(acc, ROT_HALF, 1) * sb_s[...])
                if c0 < rope_mid:
                    acc = acc * Q_SCALE
            o_ref[:, cols] = acc.astype(o_ref.dtype)

    @pl.when(j == rope_block)
    def _():
        strips(True)

    @pl.when(j != rope_block)
    def _():
        strips(False)


def _projection(x, pos, invf, w, w_dt, tm, tn):
    m, k = x.shape
    n = w.shape[1]
    rope_block = Q_OFF // tn
    assert (V_OFF - 1) // tn == rope_block and (Q_OFF - rope_block * tn) % MXU_WIDTH == 0
    kern = functools.partial(_proj_kernel, rope_block=rope_block, rope_lo=Q_OFF - rope_block * tn,
                             rope_mid=K_OFF - rope_block * tn, rope_hi=V_OFF - rope_block * tn)
    return pl.pallas_call(
        kern,
        grid=(m // tm, n // tn),
        in_specs=[pl.BlockSpec((1, 1, tm), lambda i, j: (i, 0, 0)),
                  pl.BlockSpec((SUBLANES, 1), lambda i, j: (0, 0)),
                  pl.BlockSpec((tm, k), lambda i, j: (i, 0)),
                  pl.BlockSpec((k, tn), lambda i, j: (0, j)),
                  pl.BlockSpec((k, LANES), lambda i, j: (0, 0))],
        out_specs=[pl.BlockSpec((tm, tn), lambda i, j: (i, j)),
                   pl.BlockSpec((tm, LANES), lambda i, j: (i, 0))],
        out_shape=[jax.ShapeDtypeStruct((m, n), BF16), jax.ShapeDtypeStruct((m, LANES), F32)],
        scratch_shapes=[pltpu.VMEM((tm, k), BF16), pltpu.VMEM((tm, MXU_WIDTH), F32),
                        pltpu.VMEM((tm, MXU_WIDTH), F32), pltpu.VMEM((tm, MXU_WIDTH), F32)],
        compiler_params=_params(("parallel", "arbitrary")),
        name="proj_matmul",
    )(pos.reshape(m // tm, 1, tm), invf, x, w, w_dt)


def _attn_kernel(q_ref, k_ref, v_ref, o_ref, k_s, vt_s, rhs_s, s_s):
    seq = q_ref.shape[1]
    nblk = seq // MOBA_BLOCK
    blk = MOBA_BLOCK
    lane = lax.broadcasted_iota(jnp.int32, (1, LANES), 1)
    head_of_lane = lane // ATT_HEAD_DIM

    @pl.when(pl.program_id(1) == 0)
    def _():
        blk_of_row = lax.broadcasted_iota(jnp.int32, (seq, LANES), 0) // blk
        k_s[:, LANES:] = jnp.where(blk_of_row == lane, 1.0, 0.0).astype(BF16)

    hd = ATT_HEAD_DIM
    extra = vt_s.shape[2] - hd
    ones_row = jnp.where(lax.broadcasted_iota(jnp.int32, (extra, blk), 0) == 0, 1.0, 0.0).astype(BF16)
    kmeans = []
    for j in range(nblk):
        rows = pl.ds(j * blk, blk)
        kj = k_ref[0, rows, :]
        k_s[rows, 0:LANES] = kj
        kmeans.append(jnp.sum(kj.astype(F32), axis=0, keepdims=True) * (1.0 / blk))
        vt = jnp.transpose(v_ref[0, rows, :].astype(F32)).astype(BF16)
        for h in range(2):
            vt_s[j, h, 0:hd, :] = vt[h * hd:(h + 1) * hd, :]
            vt_s[j, h, hd:, :] = ones_row
    kmean = jnp.concatenate(kmeans, axis=0)

    row_n = lax.broadcasted_iota(jnp.int32, (nblk, blk), 0)
    bias_rows = 2 * SUBLANES
    for qb in range(nblk):
        rows = pl.ds(qb * blk, blk)
        qt = jnp.transpose(q_ref[0, rows, :].astype(F32))
        for h in range(2):
            past = jnp.where(row_n < qb, 1.0, 0.0)
            if qb <= MOBA_TOPK:
                sel = past
            else:
                kmh = jnp.where(head_of_lane == h, kmean, 0.0)
                gate = jnp.dot(kmh, qt, precision=lax.Precision.HIGHEST,
                               preferred_element_type=F32)
                cnt = jnp.zeros((nblk, blk), F32)
                for m in range(qb):
                    gm = gate[m:m + 1, :]
                    ge = jnp.where(gm >= gate, 1.0, 0.0)
                    gt = jnp.where(gm > gate, 1.0, 0.0)
                    cnt = cnt + jnp.where(row_n > m, ge, jnp.where(row_n < m, gt, 0.0))
                sel = jnp.where(cnt < MOBA_TOPK, past, 0.0)
            attended = jnp.where(row_n == qb, 1.0, sel)
            bias = jnp.where(attended > 0.5, 0.0, MASK_BIAS)
            rhs_s[h, qb, h * hd:(h + 1) * hd, :] = qt[h * hd:(h + 1) * hd, :].astype(BF16)
            rhs_s[h, qb, (1 - h) * hd:(2 - h) * hd, :] = jnp.zeros((hd, blk), BF16)
            rhs_s[h, qb, LANES:LANES + bias_rows, :] = jnp.concatenate(
                [bias, jnp.zeros((bias_rows - nblk, blk), F32)], axis=0).astype(BF16)
            rhs_s[h, qb, LANES + bias_rows:, :] = jnp.zeros((LANES - bias_rows, blk), BF16)

    key_row = lax.broadcasted_iota(jnp.int32, (blk, blk), 0)
    query_col = lax.broadcasted_iota(jnp.int32, (blk, blk), 1)
    causal = key_row <= query_col

    def score_tile(qb, h, j):
        s = _dot(k_s[j * blk:(j + 1) * blk, :], rhs_s[h, qb])
        if j == qb:
            s = jnp.where(causal, s, NEG_INF)
        s_s[h, j] = s
        return jnp.max(s, axis=0, keepdims=True)

    def value_tile(h, j, m):
        p = jnp.exp2(s_s[h, j] - m)
        return _dot(vt_s[j, h], p.astype(BF16))

    def fold_max(m, mj):
        return mj if m is None else jnp.maximum(m, mj)

    units = [(qb, h) for qb in range(nblk) for h in range(2)]
    m_next = None
    for j in range(units[0][0] + 1):
        m_next = fold_max(m_next, score_tile(*units[0], j))
    outs = []
    for i, (qb, h) in enumerate(units):
        m, m_next = m_next, None
        nxt = units[i + 1] if i + 1 < len(units) else None
        todo_next = list(range(nxt[0] + 1)) if nxt else []
        acc = None
        for j in range(qb + 1):
            if todo_next:
                m_next = fold_max(m_next, score_tile(*nxt, todo_next.pop(0)))
            pv = value_tile(h, j, m)
            acc = pv if acc is None else acc + pv
        while todo_next:
            m_next = fold_max(m_next, score_tile(*nxt, todo_next.pop(0)))
        outs.append(acc[0:hd, :] / acc[hd:hd + 1, :])
        if h == 1:
            out_t = jnp.concatenate(outs, axis=0)
            o_ref[0, qb * blk:(qb + 1) * blk, :] = jnp.transpose(out_t).astype(o_ref.dtype)
            outs = []


def _attention(proj3):
    bsz, seq, _ = proj3.shape
    npairs = ATT_HEADS // 2
    nblk = seq // MOBA_BLOCK
    qkv_spec = lambda off: pl.BlockSpec((1, seq, LANES), lambda b, p: (b, 0, off // LANES + p))
    return pl.pallas_call(
        _attn_kernel,
        grid=(bsz, npairs),
        in_specs=[qkv_spec(Q_OFF), qkv_spec(K_OFF), qkv_spec(V_OFF)],
        out_specs=pl.BlockSpec((1, seq, LANES), lambda b, p: (b, 0, p)),
        out_shape=jax.ShapeDtypeStruct((bsz, seq, ATT_WIDTH), BF16),
        scratch_shapes=[pltpu.VMEM((seq, 2 * LANES), BF16),
                        pltpu.VMEM((nblk, 2, ATT_HEAD_DIM + 2 * SUBLANES, MOBA_BLOCK), BF16),
                        pltpu.VMEM((2, nblk, 2 * LANES, MOBA_BLOCK), BF16),
                        pltpu.VMEM((2, nblk, MOBA_BLOCK, MOBA_BLOCK), F32)],
        compiler_params=_params(("parallel", "arbitrary")),
        name="moba_attention",
    )(proj3, proj3, proj3)


def _split_pack(v, lane):
    hi = v.astype(BF16).astype(F32)
    lo = v - hi
    packed = jnp.where(lane < SSM_HEADS, hi, pltpu.roll(lo, SSM_HEADS, 1))
    return jnp.where(lane < 2 * SSM_HEADS, packed, 0.0).astype(BF16)


def _ssd_kernel(xbc_ref, z_ref, dt_ref, convw_ref, convb_ref, dtb_ref, alog_ref, dskip_ref,
                normw_ref, expand_ref, shift_ref, tri_ref, y_ref, tail_s, act_s, state_s):
    q = SSM_CHUNK
    lane = lax.broadcasted_iota(jnp.int32, (1, LANES), 1)

    @pl.when(pl.program_id(1) == 0)
    def _():
        tail_s[...] = jnp.zeros(tail_s.shape, F32)
        state_s[...] = jnp.zeros(state_s.shape, F32)

    strip = 4 * LANES
    sub = lax.broadcasted_iota(jnp.int32, (SUBLANES, 1), 0)
    for c0 in range(0, XBC_DIM, strip):
        cols = slice(c0, c0 + strip)
        u = xbc_ref[:, cols]
        back = _dot(shift_ref[...], u)
        conv = convb_ref[:, cols] + convw_ref[SSM_CONV - 1:SSM_CONV, cols] * u.astype(F32)
        head = conv[0:SUBLANES, :]
        tail = tail_s[:, cols]
        for d in range(1, SSM_CONV):
            w_d = convw_ref[SSM_CONV - 1 - d:SSM_CONV - d, cols]
            conv = conv + w_d * back[(d - 1) * q:d * q, :]
            head = head + w_d * (back[(d - 1) * q:(d - 1) * q + SUBLANES, :]
                                 + jnp.where(sub < d, pltpu.roll(tail, d, 0), 0.0))
        act_s[:, cols] = conv * _sigmoid(conv)
        act_s[0:SUBLANES, cols] = head * _sigmoid(head)
    tail_s[...] = xbc_ref[q - SUBLANES:q, :].astype(F32)

    dt_in = dt_ref[...] + dtb_ref[...]
    dt = jnp.maximum(dt_in, 0.0) + jnp.log1p(jnp.exp(-jnp.abs(dt_in)))
    a_dt = dt * (-jnp.exp(alog_ref[...]))
    row = lax.broadcasted_iota(jnp.int32, (q, q), 0)
    col = lax.broadcasted_iota(jnp.int32, (q, q), 1)
    causal = col <= row
    tri = tri_ref[...]
    p1 = a_dt.astype(BF16)
    r1 = a_dt - p1.astype(F32)
    p2 = r1.astype(BF16)
    p3 = (r1 - p2.astype(F32)).astype(BF16)
    acs = (_dot(tri, p1) + _dot(tri, p2) + _dot(tri, p3)) * LOG2E
    acs_t = jnp.transpose(acs)
    exp_a = jnp.exp2(acs)
    to_end = jnp.exp2(acs[q - 1:q, :] - acs)

    expand = expand_ref[...]
    dt_x = _dot(_split_pack(dt, lane), expand)
    exp_a_x = _dot(_split_pack(exp_a, lane), expand)
    to_end_x = _dot(_split_pack(to_end, lane), expand)

    pair_lane = lane // SSM_HEAD_DIM
    for g in range(SSM_GROUPS):
        cg = slice(g * GROUP_WIDTH, (g + 1) * GROUP_WIDTH)
        xs = act_s[:, cg]
        xdt = xs * dt_x[:, cg]
        b_off = D_INNER + g * SSM_STATE
        c_off = D_INNER + SSM_GROUPS * SSM_STATE + g * SSM_STATE
        bmat = act_s[:, b_off:b_off + SSM_STATE]
        cmat = act_s[:, c_off:c_off + SSM_STATE].astype(BF16)
        cb = jnp.where(causal, _dot_nt(cmat, bmat.astype(BF16)), 0.0)
        prev = state_s[g]
        y = _dot(cmat, prev.astype(BF16)) * exp_a_x[:, cg]
        y = y + xs * dskip_ref[:, cg]
        xdt_b = xdt.astype(BF16)
        diag = []
        for pr in range(HEADS_PER_GROUP // 2):
            xp = xdt_b[:, pr * LANES:(pr + 1) * LANES]
            outs = []
            for e in range(2):
                h = g * HEADS_PER_GROUP + 2 * pr + e
                seg = acs[:, h:h + 1] - acs_t[h:h + 1, :]
                decay = jnp.exp2(jnp.where(causal, seg, 0.0))
                outs.append(_dot((cb * decay).astype(BF16), xp))
            diag.append(jnp.where(pair_lane == 0, outs[0], outs[1]))
        y = y + jnp.concatenate(diag, axis=1)

        xw = (xdt * to_end_x[:, cg]).astype(BF16)
        new_state = _dot(jnp.transpose(bmat).astype(BF16), xw)
        state_s[g] = prev * exp_a_x[q - 1:q, cg] + new_state

        zg = z_ref[:, cg].astype(F32)
        hf = y * (zg * _sigmoid(zg))
        ms = jnp.mean(hf * hf, axis=1, keepdims=True)
        y_ref[:, cg] = (hf * lax.rsqrt(ms + RMS_EPS) * normw_ref[:, cg]).astype(y_ref.dtype)


def _ssd_constants():
    r = jnp.arange(LANES)[:, None]
    c = jnp.arange(D_INNER)[None, :]
    expand = ((r % SSM_HEADS) == (c // SSM_HEAD_DIM)) & (r < 2 * SSM_HEADS)
    t = jnp.arange((SSM_CONV - 1) * SSM_CHUNK)[:, None]
    s = jnp.arange(SSM_CHUNK)[None, :]
    shift = s == (t % SSM_CHUNK) - (t // SSM_CHUNK + 1)
    tri = jnp.arange(SSM_CHUNK)[None, :] <= jnp.arange(SSM_CHUNK)[:, None]
    return expand.astype(BF16), shift.astype(BF16), tri.astype(BF16)


def _ssd(proj2, dt2, bsz, seq, convw, convb, dtb, alog, dskip_x, normw, constants):
    expand, shift, tri = constants
    nc = seq // SSM_CHUNK
    q = SSM_CHUNK
    rowblk = lambda b, c: b * nc + c
    const = lambda shape: pl.BlockSpec(shape, lambda b, c: (0, 0))
    return pl.pallas_call(
        _ssd_kernel,
        grid=(bsz, nc),
        in_specs=[pl.BlockSpec((q, XBC_DIM), lambda b, c: (rowblk(b, c), XBC_OFF // XBC_DIM)),
                  pl.BlockSpec((q, D_INNER), lambda b, c: (rowblk(b, c), Z_OFF // D_INNER)),
                  pl.BlockSpec((q, LANES), lambda b, c: (rowblk(b, c), 0)),
                  const((SSM_CONV, XBC_DIM)), const((1, XBC_DIM)), const((1, LANES)),
                  const((1, LANES)), const((1, D_INNER)), const((1, D_INNER)),
                  const((LANES, D_INNER)), const(((SSM_CONV - 1) * q, q)), const((q, q))],
        out_specs=pl.BlockSpec((q, D_INNER), lambda b, c: (rowblk(b, c), 0)),
        out_shape=jax.ShapeDtypeStruct((bsz * seq, D_INNER), BF16),
        scratch_shapes=[pltpu.VMEM((SUBLANES, XBC_DIM), F32),
                        pltpu.VMEM((q, XBC_DIM), F32),
                        pltpu.VMEM((SSM_GROUPS, SSM_STATE, GROUP_WIDTH), F32)],
        compiler_params=_params(("parallel", "arbitrary")),
        name="ssd_scan",
    )(proj2, proj2, dt2, convw, convb, dtb, alog, dskip_x, normw, expand, shift, tri)


def _layer_norm(r, g, b):
    mu = jnp.mean(r, axis=1, keepdims=True)
    d = r - mu
    var = jnp.mean(d * d, axis=1, keepdims=True)
    return d * lax.rsqrt(var + LN_EPS) * g + b


def _merge_kernel(att_ref, y_ref, ga_ref, gs_ref, x_ref, wa_ref, ws_ref, wo_ref, g_ref, b_ref, o_ref,
                  wa_s, ws_s, wo_s):
    @pl.when(pl.program_id(0) == 0)
    def _():
        wa_s[...] = wa_ref[...].astype(BF16)
        ws_s[...] = ws_ref[...].astype(BF16)
        wo_s[...] = wo_ref[...].astype(BF16)

    pa = _dot(att_ref[...], wa_s[...])
    ps = _dot(y_ref[...], ws_s[...])
    merged = _sigmoid(ga_ref[...].astype(F32)) * pa + _sigmoid(gs_ref[...].astype(F32)) * ps
    mix = _dot(merged.astype(BF16), wo_s[...])
    o_ref[...] = _layer_norm(ALPHA * x_ref[...] + mix, g_ref[...], b_ref[...])


def _resident(shape):
    return pl.BlockSpec(shape, lambda i: (0, 0), pipeline_mode=pl.Buffered(1))


def _merge(att2, y2, proj2, x2, wa, ws, wo, layer, g, b, tm):
    m = x2.shape[0]
    weight = lambda rows: pl.BlockSpec((None, rows, D_MODEL), lambda i: (layer, 0, 0), pipeline_mode=pl.Buffered(1))
    return pl.pallas_call(
        _merge_kernel,
        grid=(m // tm,),
        in_specs=[pl.BlockSpec((tm, ATT_WIDTH), lambda i: (i, 0)),
                  pl.BlockSpec((tm, D_INNER), lambda i: (i, 0)),
                  pl.BlockSpec((tm, D_MODEL), lambda i: (i, GA_OFF // D_MODEL)),
                  pl.BlockSpec((tm, D_MODEL), lambda i: (i, GS_OFF // D_MODEL)),
                  pl.BlockSpec((tm, D_MODEL), lambda i: (i, 0)),
                  weight(ATT_WIDTH), weight(D_INNER), weight(D_MODEL),
                  _resident((1, D_MODEL)), _resident((1, D_MODEL))],
        out_specs=pl.BlockSpec((tm, D_MODEL), lambda i: (i, 0)),
        out_shape=jax.ShapeDtypeStruct((m, D_MODEL), F32),
        scratch_shapes=[pltpu.VMEM((ATT_WIDTH, D_MODEL), BF16), pltpu.VMEM((D_INNER, D_MODEL), BF16),
                        pltpu.VMEM((D_MODEL, D_MODEL), BF16)],
        compiler_params=_params(("arbitrary",)),
        name="merge_ln",
    )(att2, y2, proj2, proj2, x2, wa, ws, wo, g, b)


def _mlp_kernel(h_ref, wu_ref, wd_ref, g_ref, b_ref, o_ref):
    h = h_ref[...]
    up = _dot(h.astype(BF16), wu_ref[...])
    act = jnp.square(jnp.maximum(up, 0.0)).astype(BF16)
    down = _dot(act, wd_ref[...])
    o_ref[...] = _layer_norm(ALPHA * h + down, g_ref[...], b_ref[...])


def _mlp(h2, wu, wd, g, b, tm):
    m = h2.shape[0]
    return pl.pallas_call(
        _mlp_kernel,
        grid=(m // tm,),
        in_specs=[pl.BlockSpec((tm, D_MODEL), lambda i: (i, 0)),
                  _resident((D_MODEL, D_FF)), _resident((D_FF, D_MODEL)),
                  _resident((1, D_MODEL)), _resident((1, D_MODEL))],
        out_specs=pl.BlockSpec((tm, D_MODEL), lambda i: (i, 0)),
        out_shape=jax.ShapeDtypeStruct((m, D_MODEL), F32),
        compiler_params=_params(("parallel",)),
        name="mlp_ln",
    )(h2, wu, wd, g, b)


RELAYOUT_CHUNK = 1024
_MAIN_RUNS = tuple((sum(IN_SPLITS[:first]) // RELAYOUT_CHUNK, sum(IN_SPLITS[first:first + n]) // RELAYOUT_CHUNK)
                   for first, n in ((4, 1), (3, 1), (0, 3)))
_MAIN_CHUNKS = sum(n for _, n in _MAIN_RUNS)
_GATE_SRC = sum(IN_SPLITS[:6])


def _relayout_kernel(main_ref, gate_ref, o_ref):
    c = pl.program_id(0)

    @pl.when(c < _MAIN_CHUNKS)
    def _():
        o_ref[...] = main_ref[...].astype(BF16)

    @pl.when(c >= _MAIN_CHUNKS)
    def _():
        o_ref[...] = gate_ref[...].astype(BF16)


def _main_source_chunk(c):
    c = jnp.minimum(c, _MAIN_CHUNKS - 1)
    src, start = None, 0
    for first, n in _MAIN_RUNS:
        here = first + (c - start)
        src = here if src is None else jnp.where(c >= start, here, src)
        start += n
    return src


def _projection_weights(w_in, gates, layer):
    nchunks = PROJ_WIDTH // RELAYOUT_CHUNK
    blk = (None, D_MODEL, RELAYOUT_CHUNK)
    return pl.pallas_call(
        _relayout_kernel,
        grid=(nchunks,),
        in_specs=[pl.BlockSpec(blk, lambda c: (layer, 0, _main_source_chunk(c))),
                  pl.BlockSpec(blk, lambda c: (layer, 0, jnp.maximum(c - _MAIN_CHUNKS, 0)))],
        out_specs=pl.BlockSpec((D_MODEL, RELAYOUT_CHUNK), lambda c: (0, c)),
        out_shape=jax.ShapeDtypeStruct((D_MODEL, PROJ_WIDTH), BF16),
        compiler_params=_params(("arbitrary",)),
        name="proj_weight_layout",
    )(w_in, gates)


def _pad_lanes(v):
    return jnp.pad(v.astype(F32), (0, LANES - v.shape[0])).reshape(1, LANES)


def _rope_freqs():
    inv_freq = ROPE_THETA ** (-jnp.arange(0, ROT_DIM, 2, dtype=F32) / ROT_DIM)
    return inv_freq.reshape(ROT_HALF, 1)


def kernel(x, positions, ln1_g, ln1_b, w_in, conv_w, conv_b, dt_bias, a_log, d_skip, ssm_norm_w,
           w_attn_proj, w_ssm_proj, w_out, ln2_g, ln2_b, w_up, w_down):
    bsz, seq, d_model = x.shape
    assert d_model == D_MODEL and seq % MOBA_BLOCK == 0 and seq % SSM_CHUNK == 0
    assert w_in.shape == (DEPTH, D_MODEL, sum(IN_SPLITS))
    rows = bsz * seq
    tm_proj = 1024 if rows % 1024 == 0 else SSM_CHUNK
    tm_row = 512 if rows % 512 == 0 else SSM_CHUNK

    invf = _rope_freqs()
    ssd_constants = _ssd_constants()

    gates = w_in[:, :, _GATE_SRC:_GATE_SRC + 2 * D_MODEL]
    w_dt = jnp.pad(w_in[:, :, _GATE_SRC - SSM_HEADS:_GATE_SRC],
                   ((0, 0), (0, 0), (0, LANES - SSM_HEADS))).astype(BF16)

    x2 = x.reshape(rows, D_MODEL)
    for i in range(DEPTH):
        w_cat = _projection_weights(w_in, gates, i)
        proj, dt_raw = _projection(x2, positions, invf, w_cat, w_dt[i], tm_proj, PROJ_WIDTH // 4)

        att = _attention(proj.reshape(bsz, seq, PROJ_WIDTH))
        y_ssm = _ssd(proj, dt_raw, bsz, seq, conv_w[i], conv_b[i].reshape(1, XBC_DIM),
                     _pad_lanes(dt_bias[i]), _pad_lanes(a_log[i]),
                     jnp.repeat(d_skip[i].astype(F32), SSM_HEAD_DIM).reshape(1, D_INNER),
                     ssm_norm_w[i].reshape(1, D_INNER), ssd_constants)
        h = _merge(att.reshape(rows, ATT_WIDTH), y_ssm, proj, x2,
                   w_attn_proj, w_ssm_proj, w_out, i,
                   ln1_g[i].reshape(1, D_MODEL), ln1_b[i].reshape(1, D_MODEL), tm_row)
        x2 = _mlp(h, w_up[i].astype(BF16), w_down[i].astype(BF16),
                  ln2_g[i].reshape(1, D_MODEL), ln2_b[i].reshape(1, D_MODEL), tm_row)
    return x2.reshape(bsz, seq, D_MODEL)
```

```python
import functools

import jax
import jax.numpy as jnp
from jax import lax
from jax.experimental import pallas as pl
from jax.experimental.pallas import tpu as pltpu

F32 = jnp.float32
BF16 = jnp.bfloat16

D_MODEL = 1024
DEPTH = 2
ATT_HEAD_DIM = 64
ATT_HEADS = D_MODEL // ATT_HEAD_DIM
ATT_WIDTH = ATT_HEADS * ATT_HEAD_DIM
ROT_DIM = ATT_HEAD_DIM // 4
ROT_HALF = ROT_DIM // 2
ROPE_THETA = 500000.0
MOBA_BLOCK = 256
MOBA_TOPK = 3
D_INNER = 2 * D_MODEL
SSM_HEAD_DIM = 64
SSM_HEADS = D_INNER // SSM_HEAD_DIM
SSM_GROUPS = 8
SSM_STATE = 128
SSM_CONV = 4
SSM_CHUNK = 256
XBC_DIM = D_INNER + 2 * SSM_GROUPS * SSM_STATE
D_FF = 4 * D_MODEL
ALPHA = (2 * DEPTH) ** 0.25
LN_EPS = 1e-5
RMS_EPS = 1e-5
IN_SPLITS = (ATT_WIDTH, ATT_WIDTH, ATT_WIDTH, D_INNER, XBC_DIM, SSM_HEADS, D_MODEL, D_MODEL)

LANES = 128
MXU_WIDTH = 256
SUBLANES = 8
VMEM_LIMIT = 56 * 1024 * 1024

XBC_OFF = 0
Z_OFF = XBC_OFF + XBC_DIM
Q_OFF = Z_OFF + D_INNER
K_OFF = Q_OFF + ATT_WIDTH
V_OFF = K_OFF + ATT_WIDTH
GA_OFF = V_OFF + ATT_WIDTH
GS_OFF = GA_OFF + D_MODEL
PROJ_WIDTH = GS_OFF + D_MODEL

HEADS_PER_GROUP = SSM_HEADS // SSM_GROUPS
GROUP_WIDTH = HEADS_PER_GROUP * SSM_HEAD_DIM
NEG_INF = float("-inf")
MASK_BIAS = -1e30
LOG2E = 1.4426950408889634
Q_SCALE = ATT_HEAD_DIM ** -0.5 * LOG2E


def _params(semantics):
    return pltpu.CompilerParams(dimension_semantics=semantics, vmem_limit_bytes=VMEM_LIMIT)


def _dot(a, b):
    return jnp.dot(a, b, preferred_element_type=F32)


def _dot_nt(a, b, precision=None):
    return lax.dot_general(a, b, (((1,), (1,)), ((), ())), precision=precision,
                           preferred_element_type=F32)


def _sigmoid(x):
    return 1.0 / (1.0 + jnp.exp2(x * (-LOG2E)))


def _proj_kernel(pos_ref, invf_ref, x_ref, w_ref, wdt_ref, o_ref, dt_ref, xb_ref, cos_s, sa_s, sb_s,
                 *, rope_block, rope_lo, rope_mid, rope_hi):
    j = pl.program_id(1)
    tn = o_ref.shape[1]

    @pl.when(j == 0)
    def _():
        xb_ref[...] = x_ref[...].astype(BF16)
        dt_ref[...] = _dot_nt(xb_ref[...], wdt_ref[...])
        ang = invf_ref[...] * pos_ref[0].astype(F32)
        c8 = jnp.cos(ang)
        s8 = jnp.sin(ang)
        one8 = jnp.ones_like(c8)
        zero8 = jnp.zeros_like(c8)
        idle = ATT_HEAD_DIM // ROT_HALF - 2
        per_head = ((cos_s, [c8, c8] + [one8] * idle),
                    (sa_s, [-s8, zero8] + [zero8] * idle),
                    (sb_s, [zero8, s8] + [zero8] * idle))
        for table, groups in per_head:
            t = jnp.transpose(jnp.concatenate(groups * (MXU_WIDTH // ATT_HEAD_DIM), axis=0))
            table[...] = t

    def strips(with_rope):
        for c0 in range(0, tn, MXU_WIDTH):
            cols = slice(c0, min(c0 + MXU_WIDTH, tn))
            acc = _dot_nt(xb_ref[...], w_ref[cols, :])
            if with_rope and rope_lo <= c0 < rope_hi:
                acc = (acc * cos_s[...]
                       + pltpu.roll(acc, MXU_WIDTH - ROT_HALF, 1) * sa_s[...]
                       + pltpu.roll(acc, ROT_HALF, 1) * sb_s[...])
                if c0 < rope_mid:
                    acc = acc * Q_SCALE
            o_ref[:, cols] = acc.astype(o_ref.dtype)

    @pl.when(j == rope_block)
    def _():
        strips(True)

    @pl.when(j != rope_block)
    def _():
        strips(False)


def _projection(x, pos, invf, w, w_dt, tm, tn):
    m, k = x.shape
    n = w.shape[0]
    rope_block = Q_OFF // tn
    assert (V_OFF - 1) // tn == rope_block and (Q_OFF - rope_block * tn) % MXU_WIDTH == 0
    kern = functools.partial(_proj_kernel, rope_block=rope_block, rope_lo=Q_OFF - rope_block * tn,
                             rope_mid=K_OFF - rope_block * tn, rope_hi=V_OFF - rope_block * tn)
    return pl.pallas_call(
        kern,
        grid=(m // tm, n // tn),
        in_specs=[pl.BlockSpec((1, 1, tm), lambda i, j: (i, 0, 0)),
                  pl.BlockSpec((SUBLANES, 1), lambda i, j: (0, 0)),
                  pl.BlockSpec((tm, k), lambda i, j: (i, 0)),
                  pl.BlockSpec((tn, k), lambda i, j: (j, 0)),
                  pl.BlockSpec((LANES, k), lambda i, j: (0, 0))],
        out_specs=[pl.BlockSpec((tm, tn), lambda i, j: (i, j)),
                   pl.BlockSpec((tm, LANES), lambda i, j: (i, 0))],
        out_shape=[jax.ShapeDtypeStruct((m, n), BF16), jax.ShapeDtypeStruct((m, LANES), F32)],
        scratch_shapes=[pltpu.VMEM((tm, k), BF16), pltpu.VMEM((tm, MXU_WIDTH), F32),
                        pltpu.VMEM((tm, MXU_WIDTH), F32), pltpu.VMEM((tm, MXU_WIDTH), F32)],
        compiler_params=_params(("parallel", "arbitrary")),
        name="proj_matmul",
    )(pos.reshape(m // tm, 1, tm), invf, x, w, w_dt)


def _attn_kernel(q_ref, k_ref, v_ref, o_ref, k_s, vt_s, rhs_s, s_s):
    seq = q_ref.shape[1]
    nblk = seq // MOBA_BLOCK
    blk = MOBA_BLOCK
    lane = lax.broadcasted_iota(jnp.int32, (1, LANES), 1)
    head_of_lane = lane // ATT_HEAD_DIM

    @pl.when(pl.program_id(1) == 0)
    def _():
        blk_of_row = lax.broadcasted_iota(jnp.int32, (seq, LANES), 0) // blk
        k_s[:, LANES:] = jnp.where(blk_of_row == lane, 1.0, 0.0).astype(BF16)

    hd = ATT_HEAD_DIM
    extra = vt_s.shape[2] - hd
    ones_row = jnp.where(lax.broadcasted_iota(jnp.int32, (extra, blk), 0) == 0, 1.0, 0.0).astype(BF16)
    kmeans = []
    for j in range(nblk):
        rows = pl.ds(j * blk, blk)
        kj = k_ref[0, rows, :]
        k_s[rows, 0:LANES] = kj
        kmeans.append(jnp.sum(kj.astype(F32), axis=0, keepdims=True) * (1.0 / blk))
        vt = jnp.transpose(v_ref[0, rows, :].astype(F32)).astype(BF16)
        for h in range(2):
            vt_s[j, h, 0:hd, :] = vt[h * hd:(h + 1) * hd, :]
            vt_s[j, h, hd:, :] = ones_row
    kmean = jnp.concatenate(kmeans, axis=0)

    row_n = lax.broadcasted_iota(jnp.int32, (nblk, blk), 0)
    bias_rows = 2 * SUBLANES
    for qb in range(nblk):
        rows = pl.ds(qb * blk, blk)
        qt = jnp.transpose(q_ref[0, rows, :].astype(F32))
        for h in range(2):
            past = jnp.where(row_n < qb, 1.0, 0.0)
            if qb <= MOBA_TOPK:
                sel = past
            else:
                kmh = jnp.where(head_of_lane == h, kmean, 0.0)
                gate = jnp.dot(kmh, qt, precision=lax.Precision.HIGHEST,
                               preferred_element_type=F32)
                cnt = jnp.zeros((nblk, blk), F32)
                for m in range(qb):
                    gm = gate[m:m + 1, :]
                    ge = jnp.where(gm >= gate, 1.0, 0.0)
                    gt = jnp.where(gm > gate, 1.0, 0.0)
                    cnt = cnt + jnp.where(row_n > m, ge, jnp.where(row_n < m, gt, 0.0))
                sel = jnp.where(cnt < MOBA_TOPK, past, 0.0)
            attended = jnp.where(row_n == qb, 1.0, sel)
            bias = jnp.where(attended > 0.5, 0.0, MASK_BIAS)
            rhs_s[h, qb, h * hd:(h + 1) * hd, :] = qt[h * hd:(h + 1) * hd, :].astype(BF16)
            rhs_s[h, qb, (1 - h) * hd:(2 - h) * hd, :] = jnp.zeros((hd, blk), BF16)
            rhs_s[h, qb, LANES:LANES + bias_rows, :] = jnp.concatenate(
                [bias, jnp.zeros((bias_rows - nblk, blk), F32)], axis=0).astype(BF16)
            rhs_s[h, qb, LANES + bias_rows:, :] = jnp.zeros((LANES - bias_rows, blk), BF16)

    key_row = lax.broadcasted_iota(jnp.int32, (blk, blk), 0)
    query_col = lax.broadcasted_iota(jnp.int32, (blk, blk), 1)
    causal = key_row <= query_col

    def score_tile(qb, h, j):
        s = _dot(k_s[j * blk:(j + 1) * blk, :], rhs_s[h, qb])
        if j == qb:
            s = jnp.where(causal, s, NEG_INF)
        s_s[h, j] = s
        return jnp.max(s, axis=0, keepdims=True)

    def value_tile(h, j, m):
        p = jnp.exp2(s_s[h, j] - m)
        return _dot(vt_s[j, h], p.astype(BF16))

    def fold_max(m, mj):
        return mj if m is None else jnp.maximum(m, mj)

    units = [(qb, h) for qb in range(nblk) for h in range(2)]
    m_next = None
    for j in range(units[0][0] + 1):
        m_next = fold_max(m_next, score_tile(*units[0], j))
    outs = []
    for i, (qb, h) in enumerate(units):
        m, m_next = m_next, None
        nxt = units[i + 1] if i + 1 < len(units) else None
        todo_next = list(range(nxt[0] + 1)) if nxt else []
        acc = None
        for j in range(qb + 1):
            if todo_next:
                m_next = fold_max(m_next, score_tile(*nxt, todo_next.pop(0)))
            pv = value_tile(h, j, m)
            acc = pv if acc is None else acc + pv
        while todo_next:
            m_next = fold_max(m_next, score_tile(*nxt, todo_next.pop(0)))
        outs.append(acc[0:hd, :] / acc[hd:hd + 1, :])
        if h == 1:
            out_t = jnp.concatenate(outs, axis=0)
            o_ref[0, qb * blk:(qb + 1) * blk, :] = jnp.transpose(out_t).astype(o_ref.dtype)
            outs = []


def _attention(proj3):
    bsz, seq, _ = proj3.shape
    npairs = ATT_HEADS // 2
    nblk = seq // MOBA_BLOCK
    qkv_spec = lambda off: pl.BlockSpec((1, seq, LANES), lambda b, p: (b, 0, off // LANES + p))
    return pl.pallas_call(
        _attn_kernel,
        grid=(bsz, npairs),
        in_specs=[qkv_spec(Q_OFF), qkv_spec(K_OFF), qkv_spec(V_OFF)],
        out_specs=pl.BlockSpec((1, seq, LANES), lambda b, p: (b, 0, p)),
        out_shape=jax.ShapeDtypeStruct((bsz, seq, ATT_WIDTH), BF16),
        scratch_shapes=[pltpu.VMEM((seq, 2 * LANES), BF16),
                        pltpu.VMEM((nblk, 2, ATT_HEAD_DIM + 2 * SUBLANES, MOBA_BLOCK), BF16),
                        pltpu.VMEM((2, nblk, 2 * LANES, MOBA_BLOCK), BF16),
                        pltpu.VMEM((2, nblk, MOBA_BLOCK, MOBA_BLOCK), F32)],
        compiler_params=_params(("parallel", "arbitrary")),
        name="moba_attention",
    )(proj3, proj3, proj3)


def _split_pack(v, lane):
    hi = v.astype(BF16).astype(F32)
    lo = v - hi
    packed = jnp.where(lane < SSM_HEADS, hi, pltpu.roll(lo, SSM_HEADS, 1))
    return jnp.where(lane < 2 * SSM_HEADS, packed, 0.0).astype(BF16)


def _ssd_kernel(xbc_ref, z_ref, dt_ref, convw_ref, convb_ref, dtb_ref, alog_ref, dskip_ref,
                normw_ref, expand_ref, shift_ref, tri_ref, y_ref, tail_s, act_s, state_s):
    q = SSM_CHUNK
    lane = lax.broadcasted_iota(jnp.int32, (1, LANES), 1)

    @pl.when(pl.program_id(1) == 0)
    def _():
        tail_s[...] = jnp.zeros(tail_s.shape, F32)
        state_s[...] = jnp.zeros(state_s.shape, F32)

    strip = 4 * LANES
    sub = lax.broadcasted_iota(jnp.int32, (SUBLANES, 1), 0)
    for c0 in range(0, XBC_DIM, strip):
        cols = slice(c0, c0 + strip)
        u = xbc_ref[:, cols]
        back = _dot(shift_ref[...], u)
        conv = convb_ref[:, cols] + convw_ref[SSM_CONV - 1:SSM_CONV, cols] * u.astype(F32)
        head = conv[0:SUBLANES, :]
        tail = tail_s[:, cols]
        for d in range(1, SSM_CONV):
            w_d = convw_ref[SSM_CONV - 1 - d:SSM_CONV - d, cols]
            conv = conv + w_d * back[(d - 1) * q:d * q, :]
            head = head + w_d * (back[(d - 1) * q:(d - 1) * q + SUBLANES, :]
                                 + jnp.where(sub < d, pltpu.roll(tail, d, 0), 0.0))
        act_s[:, cols] = conv * _sigmoid(conv)
        act_s[0:SUBLANES, cols] = head * _sigmoid(head)
    tail_s[...] = xbc_ref[q - SUBLANES:q, :].astype(F32)

    dt_in = dt_ref[...] + dtb_ref[...]
    dt = jnp.maximum(dt_in, 0.0) + jnp.log1p(jnp.exp(-jnp.abs(dt_in)))
    a_dt = dt * (-jnp.exp(alog_ref[...]))
    row = lax.broadcasted_iota(jnp.int32, (q, q), 0)
    col = lax.broadcasted_iota(jnp.int32, (q, q), 1)
    causal = col <= row
    tri = tri_ref[...]
    p1 = a_dt.astype(BF16)
    r1 = a_dt - p1.astype(F32)
    p2 = r1.astype(BF16)
    p3 = (r1 - p2.astype(F32)).astype(BF16)
    acs = (_dot(tri, p1) + _dot(tri, p2) + _dot(tri, p3)) * LOG2E
    acs_t = jnp.transpose(acs)
    exp_a = jnp.exp2(acs)
    to_end = jnp.exp2(acs[q - 1:q, :] - acs)

    expand = expand_ref[...]
    dt_x = _dot(_split_pack(dt, lane), expand)
    exp_a_x = _dot(_split_pack(exp_a, lane), expand)
    to_end_x = _dot(_split_pack(to_end, lane), expand)

    pair_lane = lane // SSM_HEAD_DIM
    for g in range(SSM_GROUPS):
        cg = slice(g * GROUP_WIDTH, (g + 1) * GROUP_WIDTH)
        xs = act_s[:, cg]
        xdt = xs * dt_x[:, cg]
        b_off = D_INNER + g * SSM_STATE
        c_off = D_INNER + SSM_GROUPS * SSM_STATE + g * SSM_STATE
        bmat = act_s[:, b_off:b_off + SSM_STATE]
        cmat = act_s[:, c_off:c_off + SSM_STATE].astype(BF16)
        cb = jnp.where(causal, _dot_nt(cmat, bmat.astype(BF16)), 0.0)
        prev = state_s[g]
        y = _dot(cmat, prev.astype(BF16)) * exp_a_x[:, cg]
        y = y + xs * dskip_ref[:, cg]
        xdt_b = xdt.astype(BF16)
        diag = []
        for pr in range(HEADS_PER_GROUP // 2):
            xp = xdt_b[:, pr * LANES:(pr + 1) * LANES]
            outs = []
            for e in range(2):
                h = g * HEADS_PER_GROUP + 2 * pr + e
                seg = acs[:, h:h + 1] - acs_t[h:h + 1, :]
                decay = jnp.exp2(jnp.where(causal, seg, 0.0))
                outs.append(_dot((cb * decay).astype(BF16), xp))
            diag.append(jnp.where(pair_lane == 0, outs[0], outs[1]))
        y = y + jnp.concatenate(diag, axis=1)

        xw = (xdt * to_end_x[:, cg]).astype(BF16)
        new_state = _dot(jnp.transpose(bmat).astype(BF16), xw)
        state_s[g] = prev * exp_a_x[q - 1:q, cg] + new_state

        zg = z_ref[:, cg].astype(F32)
        hf = y * (zg * _sigmoid(zg))
        ms = jnp.mean(hf * hf, axis=1, keepdims=True)
        y_ref[:, cg] = (hf * lax.rsqrt(ms + RMS_EPS) * normw_ref[:, cg]).astype(y_ref.dtype)


def _ssd_constants():
    r = jnp.arange(LANES)[:, None]
    c = jnp.arange(D_INNER)[None, :]
    expand = ((r % SSM_HEADS) == (c // SSM_HEAD_DIM)) & (r < 2 * SSM_HEADS)
    t = jnp.arange((SSM_CONV - 1) * SSM_CHUNK)[:, None]
    s = jnp.arange(SSM_CHUNK)[None, :]
    shift = s == (t % SSM_CHUNK) - (t // SSM_CHUNK + 1)
    tri = jnp.arange(SSM_CHUNK)[None, :] <= jnp.arange(SSM_CHUNK)[:, None]
    return expand.astype(BF16), shift.astype(BF16), tri.astype(BF16)


def _ssd(proj2, dt2, bsz, seq, convw, convb, dtb, alog, dskip_x, normw, constants):
    expand, shift, tri = constants
    nc = seq // SSM_CHUNK
    q = SSM_CHUNK
    rowblk = lambda b, c: b * nc + c
    const = lambda shape: pl.BlockSpec(shape, lambda b, c: (0, 0))
    return pl.pallas_call(
        _ssd_kernel,
        grid=(bsz, nc),
        in_specs=[pl.BlockSpec((q, XBC_DIM), lambda b, c: (rowblk(b, c), XBC_OFF // XBC_DIM)),
                  pl.BlockSpec((q, D_INNER), lambda b, c: (rowblk(b, c), Z_OFF // D_INNER)),
                  pl.BlockSpec((q, LANES), lambda b, c: (rowblk(b, c), 0)),
                  const((SSM_CONV, XBC_DIM)), const((1, XBC_DIM)), const((1, LANES)),
                  const((1, LANES)), const((1, D_INNER)), const((1, D_INNER)),
                  const((LANES, D_INNER)), const(((SSM_CONV - 1) * q, q)), const((q, q))],
        out_specs=pl.BlockSpec((q, D_INNER), lambda b, c: (rowblk(b, c), 0)),
        out_shape=jax.ShapeDtypeStruct((bsz * seq, D_INNER), BF16),
        scratch_shapes=[pltpu.VMEM((SUBLANES, XBC_DIM), F32),
                        pltpu.VMEM((q, XBC_DIM), F32),
                        pltpu.VMEM((SSM_GROUPS, SSM_STATE, GROUP_WIDTH), F32)],
        compiler_params=_params(("parallel", "arbitrary")),
        name="ssd_scan",
    )(proj2, proj2, dt2, convw, convb, dtb, alog, dskip_x, normw, expand, shift, tri)


def _layer_norm(r, g, b):
    mu = jnp.mean(r, axis=1, keepdims=True)
    d = r - mu
    var = jnp.mean(d * d, axis=1, keepdims=True)
    return d * lax.rsqrt(var + LN_EPS) * g + b


def _merge_kernel(att_ref, y_ref, ga_ref, gs_ref, x_ref, wa_ref, ws_ref, wo_ref, g_ref, b_ref, o_ref,
                  wa_s, ws_s, wo_s):
    @pl.when(pl.program_id(0) == 0)
    def _():
        wa_s[...] = wa_ref[...].astype(BF16)
        ws_s[...] = ws_ref[...].astype(BF16)
        wo_s[...] = wo_ref[...].astype(BF16)

    pa = _dot(att_ref[...], wa_s[...])
    ps = _dot(y_ref[...], ws_s[...])
    merged = _sigmoid(ga_ref[...].astype(F32)) * pa + _sigmoid(gs_ref[...].astype(F32)) * ps
    mix = _dot(merged.astype(BF16), wo_s[...])
    o_ref[...] = _layer_norm(ALPHA * x_ref[...] + mix, g_ref[...], b_ref[...])


def _resident(shape):
    return pl.BlockSpec(shape, lambda i: (0, 0), pipeline_mode=pl.Buffered(1))


def _merge(att2, y2, proj2, x2, wa, ws, wo, layer, g, b, tm):
    m = x2.shape[0]
    weight = lambda rows: pl.BlockSpec((None, rows, D_MODEL), lambda i: (layer, 0, 0), pipeline_mode=pl.Buffered(1))
    return pl.pallas_call(
        _merge_kernel,
        grid=(m // tm,),
        in_specs=[pl.BlockSpec((tm, ATT_WIDTH), lambda i: (i, 0)),
                  pl.BlockSpec((tm, D_INNER), lambda i: (i, 0)),
                  pl.BlockSpec((tm, D_MODEL), lambda i: (i, GA_OFF // D_MODEL)),
                  pl.BlockSpec((tm, D_MODEL), lambda i: (i, GS_OFF // D_MODEL)),
                  pl.BlockSpec((tm, D_MODEL), lambda i: (i, 0)),
                  weight(ATT_WIDTH), weight(D_INNER), weight(D_MODEL),
                  _resident((1, D_MODEL)), _resident((1, D_MODEL))],
        out_specs=pl.BlockSpec((tm, D_MODEL), lambda i: (i, 0)),
        out_shape=jax.ShapeDtypeStruct((m, D_MODEL), F32),
        scratch_shapes=[pltpu.VMEM((ATT_WIDTH, D_MODEL), BF16), pltpu.VMEM((D_INNER, D_MODEL), BF16),
                        pltpu.VMEM((D_MODEL, D_MODEL), BF16)],
        compiler_params=_params(("arbitrary",)),
        name="merge_ln",
    )(att2, y2, proj2, proj2, x2, wa, ws, wo, g, b)


def _mlp_kernel(h_ref, wu_ref, wd_ref, g_ref, b_ref, o_ref):
    h = h_ref[...]
    up = _dot(h.astype(BF16), wu_ref[...])
    act = jnp.square(jnp.maximum(up, 0.0)).astype(BF16)
    down = _dot(act, wd_ref[...])
    o_ref[...] = _layer_norm(ALPHA * h + down, g_ref[...], b_ref[...])


def _mlp(h2, wu, wd, g, b, tm):
    m = h2.shape[0]
    return pl.pallas_call(
        _mlp_kernel,
        grid=(m // tm,),
        in_specs=[pl.BlockSpec((tm, D_MODEL), lambda i: (i, 0)),
                  _resident((D_MODEL, D_FF)), _resident((D_FF, D_MODEL)),
                  _resident((1, D_MODEL)), _resident((1, D_MODEL))],
        out_specs=pl.BlockSpec((tm, D_MODEL), lambda i: (i, 0)),
        out_shape=jax.ShapeDtypeStruct((m, D_MODEL), F32),
        compiler_params=_params(("parallel",)),
        name="mlp_ln",
    )(h2, wu, wd, g, b)


RELAYOUT_CHUNK = 1024
_SRC_COLUMN = tuple(sum(IN_SPLITS[:seg]) + off
                    for seg in (4, 3, 0, 1, 2, 6, 7)
                    for off in range(0, IN_SPLITS[seg], RELAYOUT_CHUNK))
_DT_SRC = sum(IN_SPLITS[:5])


def _relayout_kernel(src_ref, o_ref):
    o_ref[...] = src_ref[0].astype(BF16)


def _source_row(c):
    tile = _SRC_COLUMN[0] // SUBLANES
    for i in range(1, len(_SRC_COLUMN)):
        tile = jnp.where(c >= i, _SRC_COLUMN[i] // SUBLANES, tile)
    return tile * SUBLANES


def _projection_weights(w_t, layer):
    assert all(r % SUBLANES == 0 for r in _SRC_COLUMN) and len(_SRC_COLUMN) * RELAYOUT_CHUNK == PROJ_WIDTH
    return pl.pallas_call(
        _relayout_kernel,
        grid=(len(_SRC_COLUMN),),
        in_specs=[pl.BlockSpec((pl.Element(1), pl.Element(RELAYOUT_CHUNK), pl.Element(D_MODEL)),
                               lambda c: (layer, _source_row(c), 0))],
        out_specs=pl.BlockSpec((RELAYOUT_CHUNK, D_MODEL), lambda c: (c, 0)),
        out_shape=jax.ShapeDtypeStruct((PROJ_WIDTH, D_MODEL), BF16),
        compiler_params=_params(("parallel",)),
        name="proj_weight_layout",
    )(w_t)


def _pad_lanes(v):
    return jnp.pad(v.astype(F32), (0, LANES - v.shape[0])).reshape(1, LANES)


def _rope_freqs():
    inv_freq = ROPE_THETA ** (-jnp.arange(0, ROT_DIM, 2, dtype=F32) / ROT_DIM)
    return inv_freq.reshape(ROT_HALF, 1)


def kernel(x, positions, ln1_g, ln1_b, w_in, conv_w, conv_b, dt_bias, a_log, d_skip, ssm_norm_w,
           w_attn_proj, w_ssm_proj, w_out, ln2_g, ln2_b, w_up, w_down):
    bsz, seq, d_model = x.shape
    assert d_model == D_MODEL and seq % MOBA_BLOCK == 0 and seq % SSM_CHUNK == 0
    assert w_in.shape == (DEPTH, D_MODEL, sum(IN_SPLITS))
    rows = bsz * seq
    tm_proj = 1024 if rows % 1024 == 0 else SSM_CHUNK
    tm_row = 512 if rows % 512 == 0 else SSM_CHUNK

    invf = _rope_freqs()
    ssd_constants = _ssd_constants()

    w_t = jnp.swapaxes(w_in, 1, 2)
    w_dt = jnp.pad(w_t[:, _DT_SRC:_DT_SRC + SSM_HEADS, :], ((0, 0), (0, LANES - SSM_HEADS), (0, 0))).astype(BF16)

    x2 = x.reshape(rows, D_MODEL)
    for i in range(DEPTH):
        w_cat = _projection_weights(w_t, i)
        proj, dt_raw = _projection(x2, positions, invf, w_cat, w_dt[i], tm_proj, PROJ_WIDTH // 4)

        att = _attention(proj.reshape(bsz, seq, PROJ_WIDTH))
        y_ssm = _ssd(proj, dt_raw, bsz, seq, conv_w[i], conv_b[i].reshape(1, XBC_DIM),
                     _pad_lanes(dt_bias[i]), _pad_lanes(a_log[i]),
                     jnp.repeat(d_skip[i].astype(F32), SSM_HEAD_DIM).reshape(1, D_INNER),
                     ssm_norm_w[i].reshape(1, D_INNER), ssd_constants)
        h = _merge(att.reshape(rows, ATT_WIDTH), y_ssm, proj, x2,
                   w_attn_proj, w_ssm_proj, w_out, i,
                   ln1_g[i].reshape(1, D_MODEL), ln1_b[i].reshape(1, D_MODEL), tm_row)
        x2 = _mlp(h, w_up[i].astype(BF16), w_down[i].astype(BF16),
                  ln2_g[i].reshape(1, D_MODEL), ln2_b[i].reshape(1, D_MODEL), tm_row)
    return x2.reshape(bsz, seq, D_MODEL)
```

```python
import functools

import jax
import jax.numpy as jnp
from jax import lax
from jax.experimental import pallas as pl
from jax.experimental.pallas import tpu as pltpu

F32 = jnp.float32
BF16 = jnp.bfloat16

D_MODEL = 1024
DEPTH = 2
ATT_HEAD_DIM = 64
ATT_HEADS = D_MODEL // ATT_HEAD_DIM
ATT_WIDTH = ATT_HEADS * ATT_HEAD_DIM
ROT_DIM = ATT_HEAD_DIM // 4
ROT_HALF = ROT_DIM // 2
ROPE_THETA = 500000.0
MOBA_BLOCK = 256
MOBA_TOPK = 3
D_INNER = 2 * D_MODEL
SSM_HEAD_DIM = 64
SSM_HEADS = D_INNER // SSM_HEAD_DIM
SSM_GROUPS = 8
SSM_STATE = 128
SSM_CONV = 4
SSM_CHUNK = 256
XBC_DIM = D_INNER + 2 * SSM_GROUPS * SSM_STATE
D_FF = 4 * D_MODEL
ALPHA = (2 * DEPTH) ** 0.25
LN_EPS = 1e-5
RMS_EPS = 1e-5
IN_SPLITS = (ATT_WIDTH, ATT_WIDTH, ATT_WIDTH, D_INNER, XBC_DIM, SSM_HEADS, D_MODEL, D_MODEL)

LANES = 128
MXU_WIDTH = 256
SUBLANES = 8
VMEM_LIMIT = 56 * 1024 * 1024

XBC_OFF = 0
Z_OFF = XBC_OFF + XBC_DIM
Q_OFF = Z_OFF + D_INNER
K_OFF = Q_OFF + ATT_WIDTH
V_OFF = K_OFF + ATT_WIDTH
GA_OFF = V_OFF + ATT_WIDTH
GS_OFF = GA_OFF + D_MODEL
PROJ_WIDTH = GS_OFF + D_MODEL

HEADS_PER_GROUP = SSM_HEADS // SSM_GROUPS
GROUP_WIDTH = HEADS_PER_GROUP * SSM_HEAD_DIM
NEG_INF = float("-inf")
MASK_BIAS = -1e30
LOG2E = 1.4426950408889634
Q_SCALE = ATT_HEAD_DIM ** -0.5 * LOG2E


def _params(semantics):
    return pltpu.CompilerParams(dimension_semantics=semantics, vmem_limit_bytes=VMEM_LIMIT)


def _dot(a, b):
    return jnp.dot(a, b, preferred_element_type=F32)


def _dot_nt(a, b, precision=None):
    return lax.dot_general(a, b, (((1,), (1,)), ((), ())), precision=precision,
                           preferred_element_type=F32)


def _sigmoid(x):
    return 1.0 / (1.0 + jnp.exp2(x * (-LOG2E)))


def _proj_kernel(pos_ref, invf_ref, x_ref, w_ref, wdt_ref, convw_ref, convb_ref, o_ref, dt_ref,
                 xb_ref, cos_s, sa_s, sb_s, tail_s, raw_s, *, rows_per_seq):
    i = pl.program_id(0)
    j = pl.program_id(1)
    tm, tn = o_ref.shape

    @pl.when(j == 0)
    def _():
        xb_ref[...] = x_ref[...].astype(BF16)
        dt_ref[...] = _dot_nt(xb_ref[...], wdt_ref[...])
        ang = invf_ref[...] * pos_ref[0].astype(F32)
        c8 = jnp.cos(ang)
        s8 = jnp.sin(ang)
        one8 = jnp.ones_like(c8)
        zero8 = jnp.zeros_like(c8)
        idle = ATT_HEAD_DIM // ROT_HALF - 2
        per_head = ((cos_s, [c8, c8] + [one8] * idle),
                    (sa_s, [-s8, zero8] + [zero8] * idle),
                    (sb_s, [zero8, s8] + [zero8] * idle))
        for table, groups in per_head:
            t = jnp.transpose(jnp.concatenate(groups * (MXU_WIDTH // ATT_HEAD_DIM), axis=0))
            table[...] = t

    sub = lax.broadcasted_iota(jnp.int32, (SUBLANES, 1), 0)
    starts_sequence = (i * tm) % rows_per_seq == 0

    def conv_silu(acc, g0):
        gcols = slice(g0, g0 + MXU_WIDTH)
        tail = jnp.where(starts_sequence, 0.0, tail_s[:, gcols])
        tail_s[:, gcols] = acc[tm - SUBLANES:tm, :]
        conv = convb_ref[:, gcols] + convw_ref[SSM_CONV - 1:SSM_CONV, gcols] * acc
        for d in range(1, SSM_CONV):
            back = pltpu.roll(acc, d, 0)
            head = jnp.where(sub < d, pltpu.roll(tail, d, 0), back[0:SUBLANES, :])
            back = jnp.concatenate([head, back[SUBLANES:, :]], axis=0)
            conv = conv + convw_ref[SSM_CONV - 1 - d:SSM_CONV - d, gcols] * back
        return conv * _sigmoid(conv)

    def rope(acc):
        return (acc * cos_s[...]
                + pltpu.roll(acc, MXU_WIDTH - ROT_HALF, 1) * sa_s[...]
                + pltpu.roll(acc, ROT_HALF, 1) * sb_s[...])

    def strip_dot(c0, slot):
        raw_s[slot] = _dot_nt(xb_ref[...], w_ref[c0:c0 + MXU_WIDTH, :])

    for jj in range(PROJ_WIDTH // tn):
        @pl.when(j == jj)
        def _(jj=jj):
            strip_dot(0, 0)
            for n, c0 in enumerate(range(0, tn, MXU_WIDTH)):
                g0 = jj * tn + c0
                cols = slice(c0, c0 + MXU_WIDTH)
                if c0 + MXU_WIDTH < tn:
                    strip_dot(c0 + MXU_WIDTH, (n + 1) % 2)
                acc = raw_s[n % 2]
                if XBC_OFF <= g0 < XBC_OFF + XBC_DIM:
                    acc = conv_silu(acc, g0 - XBC_OFF)
                elif Q_OFF <= g0 < K_OFF:
                    acc = rope(acc) * Q_SCALE
                elif K_OFF <= g0 < V_OFF:
                    acc = rope(acc)
                o_ref[:, cols] = acc.astype(o_ref.dtype)


def _projection(x, pos, invf, w, w_dt, convw, convb, rows_per_seq, tm, tn):
    m, k = x.shape
    n = w.shape[0]
    assert n == PROJ_WIDTH and n % tn == 0 and tn % MXU_WIDTH == 0 and rows_per_seq % tm == 0
    assert all(off % MXU_WIDTH == 0 for off in (XBC_OFF, Z_OFF, Q_OFF, K_OFF, V_OFF))
    kern = functools.partial(_proj_kernel, rows_per_seq=rows_per_seq)
    return pl.pallas_call(
        kern,
        grid=(m // tm, n // tn),
        in_specs=[pl.BlockSpec((1, 1, tm), lambda i, j: (i, 0, 0)),
                  pl.BlockSpec((SUBLANES, 1), lambda i, j: (0, 0)),
                  pl.BlockSpec((tm, k), lambda i, j: (i, 0)),
                  pl.BlockSpec((tn, k), lambda i, j: (j, 0)),
                  pl.BlockSpec((LANES, k), lambda i, j: (0, 0)),
                  pl.BlockSpec((SSM_CONV, XBC_DIM), lambda i, j: (0, 0)),
                  pl.BlockSpec((1, XBC_DIM), lambda i, j: (0, 0))],
        out_specs=[pl.BlockSpec((tm, tn), lambda i, j: (i, j)),
                   pl.BlockSpec((tm, LANES), lambda i, j: (i, 0))],
        out_shape=[jax.ShapeDtypeStruct((m, n), BF16), jax.ShapeDtypeStruct((m, LANES), F32)],
        scratch_shapes=[pltpu.VMEM((tm, k), BF16), pltpu.VMEM((tm, MXU_WIDTH), F32),
                        pltpu.VMEM((tm, MXU_WIDTH), F32), pltpu.VMEM((tm, MXU_WIDTH), F32),
                        pltpu.VMEM((SUBLANES, XBC_DIM), F32), pltpu.VMEM((2, tm, MXU_WIDTH), F32)],
        compiler_params=_params(("arbitrary", "arbitrary")),
        name="proj_matmul",
    )(pos.reshape(m // tm, 1, tm), invf, x, w, w_dt, convw, convb)


def _attn_kernel(q_ref, k_ref, v_ref, o_ref, k_s, vt_s, rhs_s, s_s):
    seq = q_ref.shape[1]
    nblk = seq // MOBA_BLOCK
    blk = MOBA_BLOCK
    lane = lax.broadcasted_iota(jnp.int32, (1, LANES), 1)
    head_of_lane = lane // ATT_HEAD_DIM

    @pl.when(pl.program_id(1) == 0)
    def _():
        blk_of_row = lax.broadcasted_iota(jnp.int32, (seq, LANES), 0) // blk
        k_s[:, LANES:] = jnp.where(blk_of_row == lane, 1.0, 0.0).astype(BF16)

    hd = ATT_HEAD_DIM
    extra = vt_s.shape[2] - hd
    ones_row = jnp.where(lax.broadcasted_iota(jnp.int32, (extra, blk), 0) == 0, 1.0, 0.0).astype(BF16)
    kmeans = []
    for j in range(nblk):
        rows = pl.ds(j * blk, blk)
        kj = k_ref[0, rows, :]
        k_s[rows, 0:LANES] = kj
        kmeans.append(jnp.sum(kj.astype(F32), axis=0, keepdims=True) * (1.0 / blk))
        vt = jnp.transpose(v_ref[0, rows, :].astype(F32)).astype(BF16)
        for h in range(2):
            vt_s[j, h, 0:hd, :] = vt[h * hd:(h + 1) * hd, :]
            vt_s[j, h, hd:, :] = ones_row
    kmean = jnp.concatenate(kmeans, axis=0)

    row_n = lax.broadcasted_iota(jnp.int32, (nblk, blk), 0)
    bias_rows = 2 * SUBLANES
    for qb in range(nblk):
        rows = pl.ds(qb * blk, blk)
        qt = jnp.transpose(q_ref[0, rows, :].astype(F32))
        for h in range(2):
            past = jnp.where(row_n < qb, 1.0, 0.0)
            if qb <= MOBA_TOPK:
                sel = past
            else:
                kmh = jnp.where(head_of_lane == h, kmean, 0.0)
                gate = jnp.dot(kmh, qt, precision=lax.Precision.HIGHEST,
                               preferred_element_type=F32)
                cnt = jnp.zeros((nblk, blk), F32)
                for m in range(qb):
                    gm = gate[m:m + 1, :]
                    ge = jnp.where(gm >= gate, 1.0, 0.0)
                    gt = jnp.where(gm > gate, 1.0, 0.0)
                    cnt = cnt + jnp.where(row_n > m, ge, jnp.where(row_n < m, gt, 0.0))
                sel = jnp.where(cnt < MOBA_TOPK, past, 0.0)
            attended = jnp.where(row_n == qb, 1.0, sel)
            bias = jnp.where(attended > 0.5, 0.0, MASK_BIAS)
            rhs_s[h, qb, h * hd:(h + 1) * hd, :] = qt[h * hd:(h + 1) * hd, :].astype(BF16)
            rhs_s[h, qb, (1 - h) * hd:(2 - h) * hd, :] = jnp.zeros((hd, blk), BF16)
            rhs_s[h, qb, LANES:LANES + bias_rows, :] = jnp.concatenate(
                [bias, jnp.zeros((bias_rows - nblk, blk), F32)], axis=0).astype(BF16)
            rhs_s[h, qb, LANES + bias_rows:, :] = jnp.zeros((LANES - bias_rows, blk), BF16)

    key_row = lax.broadcasted_iota(jnp.int32, (blk, blk), 0)
    query_col = lax.broadcasted_iota(jnp.int32, (blk, blk), 1)
    causal = key_row <= query_col

    def score_tile(qb, h, j):
        s = _dot(k_s[j * blk:(j + 1) * blk, :], rhs_s[h, qb])
        if j == qb:
            s = jnp.where(causal, s, NEG_INF)
        s_s[h, j] = s
        return jnp.max(s, axis=0, keepdims=True)

    def value_tile(h, j, m):
        p = jnp.exp2(s_s[h, j] - m)
        return _dot(vt_s[j, h], p.astype(BF16))

    def fold_max(m, mj):
        return mj if m is None else jnp.maximum(m, mj)

    units = [(qb, h) for qb in range(nblk) for h in range(2)]
    m_next = None
    for j in range(units[0][0] + 1):
        m_next = fold_max(m_next, score_tile(*units[0], j))
    outs = []
    for i, (qb, h) in enumerate(units):
        m, m_next = m_next, None
        nxt = units[i + 1] if i + 1 < len(units) else None
        todo_next = list(range(nxt[0] + 1)) if nxt else []
        acc = None
        for j in range(qb + 1):
            if todo_next:
                m_next = fold_max(m_next, score_tile(*nxt, todo_next.pop(0)))
            pv = value_tile(h, j, m)
            acc = pv if acc is None else acc + pv
        while todo_next:
            m_next = fold_max(m_next, score_tile(*nxt, todo_next.pop(0)))
        outs.append(acc[0:hd, :] / acc[hd:hd + 1, :])
        if h == 1:
            out_t = jnp.concatenate(outs, axis=0)
            o_ref[0, qb * blk:(qb + 1) * blk, :] = jnp.transpose(out_t).astype(o_ref.dtype)
            outs = []


def _attention(proj3):
    bsz, seq, _ = proj3.shape
    npairs = ATT_HEADS // 2
    nblk = seq // MOBA_BLOCK
    qkv_spec = lambda off: pl.BlockSpec((1, seq, LANES), lambda b, p: (b, 0, off // LANES + p))
    return pl.pallas_call(
        _attn_kernel,
        grid=(bsz, npairs),
        in_specs=[qkv_spec(Q_OFF), qkv_spec(K_OFF), qkv_spec(V_OFF)],
        out_specs=pl.BlockSpec((1, seq, LANES), lambda b, p: (b, 0, p)),
        out_shape=jax.ShapeDtypeStruct((bsz, seq, ATT_WIDTH), BF16),
        scratch_shapes=[pltpu.VMEM((seq, 2 * LANES), BF16),
                        pltpu.VMEM((nblk, 2, ATT_HEAD_DIM + 2 * SUBLANES, MOBA_BLOCK), BF16),
                        pltpu.VMEM((2, nblk, 2 * LANES, MOBA_BLOCK), BF16),
                        pltpu.VMEM((2, nblk, MOBA_BLOCK, MOBA_BLOCK), F32)],
        compiler_params=_params(("parallel", "arbitrary")),
        name="moba_attention",
    )(proj3, proj3, proj3)


def _split_pack(v, lane):
    hi = v.astype(BF16).astype(F32)
    lo = v - hi
    packed = jnp.where(lane < SSM_HEADS, hi, pltpu.roll(lo, SSM_HEADS, 1))
    return jnp.where(lane < 2 * SSM_HEADS, packed, 0.0).astype(BF16)


def _ssd_kernel(xbc_ref, z_ref, dt_ref, dtb_ref, alog_ref, dskip_ref,
                normw_ref, expand_ref, tri_ref, y_ref, state_s):
    q = SSM_CHUNK
    lane = lax.broadcasted_iota(jnp.int32, (1, LANES), 1)

    @pl.when(pl.program_id(1) == 0)
    def _():
        state_s[...] = jnp.zeros(state_s.shape, F32)

    dt_in = dt_ref[...] + dtb_ref[...]
    dt = jnp.maximum(dt_in, 0.0) + jnp.log1p(jnp.exp(-jnp.abs(dt_in)))
    a_dt = dt * (-jnp.exp(alog_ref[...]))
    row = lax.broadcasted_iota(jnp.int32, (q, q), 0)
    col = lax.broadcasted_iota(jnp.int32, (q, q), 1)
    causal = col <= row
    tri = tri_ref[...]
    p1 = a_dt.astype(BF16)
    r1 = a_dt - p1.astype(F32)
    p2 = r1.astype(BF16)
    p3 = (r1 - p2.astype(F32)).astype(BF16)
    acs = (_dot(tri, p1) + _dot(tri, p2) + _dot(tri, p3)) * LOG2E
    acs_t = jnp.transpose(acs)
    exp_a = jnp.exp2(acs)
    to_end = jnp.exp2(acs[q - 1:q, :] - acs)

    expand = expand_ref[...]
    dt_x = _dot(_split_pack(dt, lane), expand)
    exp_a_x = _dot(_split_pack(exp_a, lane), expand)
    to_end_x = _dot(_split_pack(to_end, lane), expand)

    pair_lane = lane // SSM_HEAD_DIM
    for g in range(SSM_GROUPS):
        cg = slice(g * GROUP_WIDTH, (g + 1) * GROUP_WIDTH)
        xs = xbc_ref[:, cg].astype(F32)
        xdt = xs * dt_x[:, cg]
        b_off = D_INNER + g * SSM_STATE
        c_off = D_INNER + SSM_GROUPS * SSM_STATE + g * SSM_STATE
        bmat = xbc_ref[:, b_off:b_off + SSM_STATE]
        cmat = xbc_ref[:, c_off:c_off + SSM_STATE]
        cb = jnp.where(causal, _dot_nt(cmat, bmat), 0.0)
        prev = state_s[g]
        y = _dot(cmat, prev.astype(BF16)) * exp_a_x[:, cg]
        y = y + xs * dskip_ref[:, cg]
        xdt_b = xdt.astype(BF16)
        diag = []
        for pr in range(HEADS_PER_GROUP // 2):
            xp = xdt_b[:, pr * LANES:(pr + 1) * LANES]
            outs = []
            for e in range(2):
                h = g * HEADS_PER_GROUP + 2 * pr + e
                seg = acs[:, h:h + 1] - acs_t[h:h + 1, :]
                decay = jnp.exp2(jnp.where(causal, seg, 0.0))
                outs.append(_dot((cb * decay).astype(BF16), xp))
            diag.append(jnp.where(pair_lane == 0, outs[0], outs[1]))
        y = y + jnp.concatenate(diag, axis=1)

        xw = (xdt * to_end_x[:, cg]).astype(BF16)
        new_state = _dot(jnp.transpose(bmat.astype(F32)).astype(BF16), xw)
        state_s[g] = prev * exp_a_x[q - 1:q, cg] + new_state

        zg = z_ref[:, cg].astype(F32)
        hf = y * (zg * _sigmoid(zg))
        ms = jnp.mean(hf * hf, axis=1, keepdims=True)
        y_ref[:, cg] = (hf * lax.rsqrt(ms + RMS_EPS) * normw_ref[:, cg]).astype(y_ref.dtype)


def _ssd_constants():
    r = jnp.arange(LANES)[:, None]
    c = jnp.arange(D_INNER)[None, :]
    expand = ((r % SSM_HEADS) == (c // SSM_HEAD_DIM)) & (r < 2 * SSM_HEADS)
    tri = jnp.arange(SSM_CHUNK)[None, :] <= jnp.arange(SSM_CHUNK)[:, None]
    return expand.astype(BF16), tri.astype(BF16)


def _ssd(proj2, dt2, bsz, seq, dtb, alog, dskip_x, normw, constants):
    expand, tri = constants
    nc = seq // SSM_CHUNK
    q = SSM_CHUNK
    rowblk = lambda b, c: b * nc + c
    const = lambda shape: pl.BlockSpec(shape, lambda b, c: (0, 0))
    return pl.pallas_call(
        _ssd_kernel,
        grid=(bsz, nc),
        in_specs=[pl.BlockSpec((q, XBC_DIM), lambda b, c: (rowblk(b, c), XBC_OFF // XBC_DIM)),
                  pl.BlockSpec((q, D_INNER), lambda b, c: (rowblk(b, c), Z_OFF // D_INNER)),
                  pl.BlockSpec((q, LANES), lambda b, c: (rowblk(b, c), 0)),
                  const((1, LANES)), const((1, LANES)), const((1, D_INNER)), const((1, D_INNER)),
                  const((LANES, D_INNER)), const((q, q))],
        out_specs=pl.BlockSpec((q, D_INNER), lambda b, c: (rowblk(b, c), 0)),
        out_shape=jax.ShapeDtypeStruct((bsz * seq, D_INNER), BF16),
        scratch_shapes=[pltpu.VMEM((SSM_GROUPS, SSM_STATE, GROUP_WIDTH), F32)],
        compiler_params=_params(("parallel", "arbitrary")),
        name="ssd_scan",
    )(proj2, proj2, dt2, dtb, alog, dskip_x, normw, expand, tri)


def _layer_norm(r, g, b):
    mu = jnp.mean(r, axis=1, keepdims=True)
    d = r - mu
    var = jnp.mean(d * d, axis=1, keepdims=True)
    return d * lax.rsqrt(var + LN_EPS) * g + b


def _merge_kernel(att_ref, y_ref, ga_ref, gs_ref, x_ref, wa_ref, ws_ref, wo_ref, g_ref, b_ref, o_ref,
                  wa_s, ws_s, wo_s):
    @pl.when(pl.program_id(0) == 0)
    def _():
        wa_s[...] = wa_ref[...].astype(BF16)
        ws_s[...] = ws_ref[...].astype(BF16)
        wo_s[...] = wo_ref[...].astype(BF16)

    pa = _dot(att_ref[...], wa_s[...])
    ps = _dot(y_ref[...], ws_s[...])
    merged = _sigmoid(ga_ref[...].astype(F32)) * pa + _sigmoid(gs_ref[...].astype(F32)) * ps
    mix = _dot(merged.astype(BF16), wo_s[...])
    o_ref[...] = _layer_norm(ALPHA * x_ref[...] + mix, g_ref[...], b_ref[...])


def _resident(shape):
    return pl.BlockSpec(shape, lambda i: (0, 0), pipeline_mode=pl.Buffered(1))


def _merge(att2, y2, proj2, x2, wa, ws, wo, layer, g, b, tm):
    m = x2.shape[0]
    weight = lambda rows: pl.BlockSpec((None, rows, D_MODEL), lambda i: (layer, 0, 0), pipeline_mode=pl.Buffered(1))
    return pl.pallas_call(
        _merge_kernel,
        grid=(m // tm,),
        in_specs=[pl.BlockSpec((tm, ATT_WIDTH), lambda i: (i, 0)),
                  pl.BlockSpec((tm, D_INNER), lambda i: (i, 0)),
                  pl.BlockSpec((tm, D_MODEL), lambda i: (i, GA_OFF // D_MODEL)),
                  pl.BlockSpec((tm, D_MODEL), lambda i: (i, GS_OFF // D_MODEL)),
                  pl.BlockSpec((tm, D_MODEL), lambda i: (i, 0)),
                  weight(ATT_WIDTH), weight(D_INNER), weight(D_MODEL),
                  _resident((1, D_MODEL)), _resident((1, D_MODEL))],
        out_specs=pl.BlockSpec((tm, D_MODEL), lambda i: (i, 0)),
        out_shape=jax.ShapeDtypeStruct((m, D_MODEL), F32),
        scratch_shapes=[pltpu.VMEM((ATT_WIDTH, D_MODEL), BF16), pltpu.VMEM((D_INNER, D_MODEL), BF16),
                        pltpu.VMEM((D_MODEL, D_MODEL), BF16)],
        compiler_params=_params(("arbitrary",)),
        name="merge_ln",
    )(att2, y2, proj2, proj2, x2, wa, ws, wo, g, b)


def _mlp_kernel(h_ref, wu_ref, wd_ref, g_ref, b_ref, o_ref):
    h = h_ref[...]
    up = _dot(h.astype(BF16), wu_ref[...])
    act = jnp.square(jnp.maximum(up, 0.0)).astype(BF16)
    down = _dot(act, wd_ref[...])
    o_ref[...] = _layer_norm(ALPHA * h + down, g_ref[...], b_ref[...])


def _mlp(h2, wu, wd, g, b, tm):
    m = h2.shape[0]
    return pl.pallas_call(
        _mlp_kernel,
        grid=(m // tm,),
        in_specs=[pl.BlockSpec((tm, D_MODEL), lambda i: (i, 0)),
                  _resident((D_MODEL, D_FF)), _resident((D_FF, D_MODEL)),
                  _resident((1, D_MODEL)), _resident((1, D_MODEL))],
        out_specs=pl.BlockSpec((tm, D_MODEL), lambda i: (i, 0)),
        out_shape=jax.ShapeDtypeStruct((m, D_MODEL), F32),
        compiler_params=_params(("parallel",)),
        name="mlp_ln",
    )(h2, wu, wd, g, b)


RELAYOUT_CHUNK = 1024
_SRC_COLUMN = tuple(sum(IN_SPLITS[:seg]) + off
                    for seg in (4, 3, 0, 1, 2, 6, 7)
                    for off in range(0, IN_SPLITS[seg], RELAYOUT_CHUNK))
_DT_SRC = sum(IN_SPLITS[:5])


def _relayout_kernel(src_ref, o_ref):
    o_ref[...] = src_ref[0].astype(BF16)


def _source_row(c):
    tile = _SRC_COLUMN[0] // SUBLANES
    for i in range(1, len(_SRC_COLUMN)):
        tile = jnp.where(c >= i, _SRC_COLUMN[i] // SUBLANES, tile)
    return tile * SUBLANES


def _projection_weights(w_t, layer):
    assert all(r % SUBLANES == 0 for r in _SRC_COLUMN) and len(_SRC_COLUMN) * RELAYOUT_CHUNK == PROJ_WIDTH
    return pl.pallas_call(
        _relayout_kernel,
        grid=(len(_SRC_COLUMN),),
        in_specs=[pl.BlockSpec((pl.Element(1), pl.Element(RELAYOUT_CHUNK), pl.Element(D_MODEL)),
                               lambda c: (layer, _source_row(c), 0))],
        out_specs=pl.BlockSpec((RELAYOUT_CHUNK, D_MODEL), lambda c: (c, 0)),
        out_shape=jax.ShapeDtypeStruct((PROJ_WIDTH, D_MODEL), BF16),
        compiler_params=_params(("parallel",)),
        name="proj_weight_layout",
    )(w_t)


def _pad_lanes(v):
    return jnp.pad(v.astype(F32), (0, LANES - v.shape[0])).reshape(1, LANES)


def _rope_freqs():
    inv_freq = ROPE_THETA ** (-jnp.arange(0, ROT_DIM, 2, dtype=F32) / ROT_DIM)
    return inv_freq.reshape(ROT_HALF, 1)


def kernel(x, positions, ln1_g, ln1_b, w_in, conv_w, conv_b, dt_bias, a_log, d_skip, ssm_norm_w,
           w_attn_proj, w_ssm_proj, w_out, ln2_g, ln2_b, w_up, w_down):
    bsz, seq, d_model = x.shape
    assert d_model == D_MODEL and seq % MOBA_BLOCK == 0 and seq % SSM_CHUNK == 0
    assert w_in.shape == (DEPTH, D_MODEL, sum(IN_SPLITS))
    rows = bsz * seq
    tm_proj = 1024 if seq % 1024 == 0 else SSM_CHUNK
    tm_row = 512 if rows % 512 == 0 else SSM_CHUNK

    invf = _rope_freqs()
    ssd_constants = _ssd_constants()

    w_t = jnp.swapaxes(w_in, 1, 2)
    w_dt = jnp.pad(w_t[:, _DT_SRC:_DT_SRC + SSM_HEADS, :], ((0, 0), (0, LANES - SSM_HEADS), (0, 0))).astype(BF16)

    x2 = x.reshape(rows, D_MODEL)
    for i in range(DEPTH):
        w_cat = _projection_weights(w_t, i)
        proj, dt_raw = _projection(x2, positions, invf, w_cat, w_dt[i], conv_w[i], conv_b[i].reshape(1, XBC_DIM),
                                   seq, tm_proj, PROJ_WIDTH // 4)

        att = _attention(proj.reshape(bsz, seq, PROJ_WIDTH))
        y_ssm = _ssd(proj, dt_raw, bsz, seq, _pad_lanes(dt_bias[i]), _pad_lanes(a_log[i]),
                     jnp.repeat(d_skip[i].astype(F32), SSM_HEAD_DIM).reshape(1, D_INNER),
                     ssm_norm_w[i].reshape(1, D_INNER), ssd_constants)
        h = _merge(att.reshape(rows, ATT_WIDTH), y_ssm, proj, x2,
                   w_attn_proj, w_ssm_proj, w_out, i,
                   ln1_g[i].reshape(1, D_MODEL), ln1_b[i].reshape(1, D_MODEL), tm_row)
        x2 = _mlp(h, w_up[i].astype(BF16), w_down[i].astype(BF16),
                  ln2_g[i].reshape(1, D_MODEL), ln2_b[i].reshape(1, D_MODEL), tm_row)
    return x2.reshape(bsz, seq, D_MODEL)
```

```python
import functools

import jax
import jax.numpy as jnp
from jax import lax
from jax.experimental import pallas as pl
from jax.experimental.pallas import tpu as pltpu

F32 = jnp.float32
BF16 = jnp.bfloat16

D_MODEL = 1024
DEPTH = 2
ATT_HEAD_DIM = 64
ATT_HEADS = D_MODEL // ATT_HEAD_DIM
ATT_WIDTH = ATT_HEADS * ATT_HEAD_DIM
ROT_DIM = ATT_HEAD_DIM // 4
ROT_HALF = ROT_DIM // 2
ROPE_THETA = 500000.0
MOBA_BLOCK = 256
MOBA_TOPK = 3
D_INNER = 2 * D_MODEL
SSM_HEAD_DIM = 64
SSM_HEADS = D_INNER // SSM_HEAD_DIM
SSM_GROUPS = 8
SSM_STATE = 128
SSM_CONV = 4
SSM_CHUNK = 256
XBC_DIM = D_INNER + 2 * SSM_GROUPS * SSM_STATE
D_FF = 4 * D_MODEL
ALPHA = (2 * DEPTH) ** 0.25
LN_EPS = 1e-5
RMS_EPS = 1e-5
IN_SPLITS = (ATT_WIDTH, ATT_WIDTH, ATT_WIDTH, D_INNER, XBC_DIM, SSM_HEADS, D_MODEL, D_MODEL)

LANES = 128
MXU_WIDTH = 256
SUBLANES = 8
VMEM_LIMIT = 56 * 1024 * 1024

XBC_OFF = 0
Z_OFF = XBC_OFF + XBC_DIM
Q_OFF = Z_OFF + D_INNER
K_OFF = Q_OFF + ATT_WIDTH
V_OFF = K_OFF + ATT_WIDTH
GA_OFF = V_OFF + ATT_WIDTH
GS_OFF = GA_OFF + D_MODEL
PROJ_WIDTH = GS_OFF + D_MODEL

HEADS_PER_GROUP = SSM_HEADS // SSM_GROUPS
GROUP_WIDTH = HEADS_PER_GROUP * SSM_HEAD_DIM
NEG_INF = float("-inf")
MASK_BIAS = -1e30
LOG2E = 1.4426950408889634
Q_SCALE = ATT_HEAD_DIM ** -0.5 * LOG2E
SCORE_UNITS_AHEAD = 3
SCORE_SLOTS = SCORE_UNITS_AHEAD + 1


def _params(semantics):
    return pltpu.CompilerParams(dimension_semantics=semantics, vmem_limit_bytes=VMEM_LIMIT)


def _dot(a, b):
    return jnp.dot(a, b, preferred_element_type=F32)


def _dot_nt(a, b, precision=None):
    return lax.dot_general(a, b, (((1,), (1,)), ((), ())), precision=precision,
                           preferred_element_type=F32)


def _sigmoid(x):
    return 1.0 / (1.0 + jnp.exp2(x * (-LOG2E)))


def _proj_kernel(pos_ref, invf_ref, x_ref, w_ref, wdt_ref, convw_ref, convb_ref, o_ref, dt_ref,
                 xb_ref, cos_s, sa_s, sb_s, tail_s, raw_s, *, rows_per_seq):
    i = pl.program_id(0)
    j = pl.program_id(1)
    tm, tn = o_ref.shape

    @pl.when(j == 0)
    def _():
        xb_ref[...] = x_ref[...].astype(BF16)
        dt_ref[...] = _dot_nt(xb_ref[...], wdt_ref[...])
        ang = invf_ref[...] * pos_ref[0].astype(F32)
        c8 = jnp.cos(ang)
        s8 = jnp.sin(ang)
        one8 = jnp.ones_like(c8)
        zero8 = jnp.zeros_like(c8)
        idle = ATT_HEAD_DIM // ROT_HALF - 2
        per_head = ((cos_s, [c8, c8] + [one8] * idle),
                    (sa_s, [-s8, zero8] + [zero8] * idle),
                    (sb_s, [zero8, s8] + [zero8] * idle))
        for table, groups in per_head:
            t = jnp.transpose(jnp.concatenate(groups * (MXU_WIDTH // ATT_HEAD_DIM), axis=0))
            table[...] = t

    sub = lax.broadcasted_iota(jnp.int32, (SUBLANES, 1), 0)
    starts_sequence = (i * tm) % rows_per_seq == 0

    def conv_silu(acc, g0):
        gcols = slice(g0, g0 + MXU_WIDTH)
        tail = jnp.where(starts_sequence, 0.0, tail_s[:, gcols])
        tail_s[:, gcols] = acc[tm - SUBLANES:tm, :]
        conv = convb_ref[:, gcols] + convw_ref[SSM_CONV - 1:SSM_CONV, gcols] * acc
        for d in range(1, SSM_CONV):
            back = pltpu.roll(acc, d, 0)
            head = jnp.where(sub < d, pltpu.roll(tail, d, 0), back[0:SUBLANES, :])
            back = jnp.concatenate([head, back[SUBLANES:, :]], axis=0)
            conv = conv + convw_ref[SSM_CONV - 1 - d:SSM_CONV - d, gcols] * back
        return conv * _sigmoid(conv)

    def rope(acc):
        return (acc * cos_s[...]
                + pltpu.roll(acc, MXU_WIDTH - ROT_HALF, 1) * sa_s[...]
                + pltpu.roll(acc, ROT_HALF, 1) * sb_s[...])

    def strip_dot(c0, slot):
        raw_s[slot] = _dot_nt(xb_ref[...], w_ref[c0:c0 + MXU_WIDTH, :])

    for jj in range(PROJ_WIDTH // tn):
        @pl.when(j == jj)
        def _(jj=jj):
            strip_dot(0, 0)
            for n, c0 in enumerate(range(0, tn, MXU_WIDTH)):
                g0 = jj * tn + c0
                cols = slice(c0, c0 + MXU_WIDTH)
                if c0 + MXU_WIDTH < tn:
                    strip_dot(c0 + MXU_WIDTH, (n + 1) % 2)
                acc = raw_s[n % 2]
                if XBC_OFF <= g0 < XBC_OFF + XBC_DIM:
                    acc = conv_silu(acc, g0 - XBC_OFF)
                elif Q_OFF <= g0 < K_OFF:
                    acc = rope(acc) * Q_SCALE
                elif K_OFF <= g0 < V_OFF:
                    acc = rope(acc)
                o_ref[:, cols] = acc.astype(o_ref.dtype)


def _projection(x, pos, invf, w, w_dt, convw, convb, rows_per_seq, tm, tn):
    m, k = x.shape
    n = w.shape[0]
    assert n == PROJ_WIDTH and n % tn == 0 and tn % MXU_WIDTH == 0 and rows_per_seq % tm == 0
    assert all(off % MXU_WIDTH == 0 for off in (XBC_OFF, Z_OFF, Q_OFF, K_OFF, V_OFF))
    kern = functools.partial(_proj_kernel, rows_per_seq=rows_per_seq)
    return pl.pallas_call(
        kern,
        grid=(m // tm, n // tn),
        in_specs=[pl.BlockSpec((1, 1, tm), lambda i, j: (i, 0, 0)),
                  pl.BlockSpec((SUBLANES, 1), lambda i, j: (0, 0)),
                  pl.BlockSpec((tm, k), lambda i, j: (i, 0)),
                  pl.BlockSpec((tn, k), lambda i, j: (j, 0)),
                  pl.BlockSpec((LANES, k), lambda i, j: (0, 0)),
                  pl.BlockSpec((SSM_CONV, XBC_DIM), lambda i, j: (0, 0)),
                  pl.BlockSpec((1, XBC_DIM), lambda i, j: (0, 0))],
        out_specs=[pl.BlockSpec((tm, tn), lambda i, j: (i, j)),
                   pl.BlockSpec((tm, LANES), lambda i, j: (i, 0))],
        out_shape=[jax.ShapeDtypeStruct((m, n), BF16), jax.ShapeDtypeStruct((m, LANES), F32)],
        scratch_shapes=[pltpu.VMEM((tm, k), BF16), pltpu.VMEM((tm, MXU_WIDTH), F32),
                        pltpu.VMEM((tm, MXU_WIDTH), F32), pltpu.VMEM((tm, MXU_WIDTH), F32),
                        pltpu.VMEM((SUBLANES, XBC_DIM), F32), pltpu.VMEM((2, tm, MXU_WIDTH), F32)],
        compiler_params=_params(("arbitrary", "arbitrary")),
        name="proj_matmul",
    )(pos.reshape(m // tm, 1, tm), invf, x, w, w_dt, convw, convb)


def _attn_kernel(q_ref, k_ref, v_ref, o_ref, k_s, vt_s, rhs_s, s_s):
    seq = q_ref.shape[1]
    nblk = seq // MOBA_BLOCK
    blk = MOBA_BLOCK
    lane = lax.broadcasted_iota(jnp.int32, (1, LANES), 1)
    head_of_lane = lane // ATT_HEAD_DIM

    @pl.when(pl.program_id(1) == 0)
    def _():
        blk_of_row = lax.broadcasted_iota(jnp.int32, (seq, LANES), 0) // blk
        k_s[:, LANES:] = jnp.where(blk_of_row == lane, 1.0, 0.0).astype(BF16)

    hd = ATT_HEAD_DIM
    extra = vt_s.shape[2] - hd
    ones_row = jnp.where(lax.broadcasted_iota(jnp.int32, (extra, blk), 0) == 0, 1.0, 0.0).astype(BF16)
    kmeans = []
    for j in range(nblk):
        rows = pl.ds(j * blk, blk)
        kj = k_ref[0, rows, :]
        k_s[rows, 0:LANES] = kj
        kmeans.append(jnp.sum(kj.astype(F32), axis=0, keepdims=True) * (1.0 / blk))
        vt = jnp.transpose(v_ref[0, rows, :].astype(F32)).astype(BF16)
        for h in range(2):
            vt_s[j, h, 0:hd, :] = vt[h * hd:(h + 1) * hd, :]
            vt_s[j, h, hd:, :] = ones_row
    kmean = jnp.concatenate(kmeans, axis=0)

    row_n = lax.broadcasted_iota(jnp.int32, (nblk, blk), 0)
    bias_rows = 2 * SUBLANES
    for qb in range(nblk):
        rows = pl.ds(qb * blk, blk)
        qt = jnp.transpose(q_ref[0, rows, :].astype(F32))
        for h in range(2):
            past = jnp.where(row_n < qb, 1.0, 0.0)
            if qb <= MOBA_TOPK:
                sel = past
            else:
                kmh = jnp.where(head_of_lane == h, kmean, 0.0)
                gate = jnp.dot(kmh, qt, precision=lax.Precision.HIGHEST,
                               preferred_element_type=F32)
                cnt = jnp.zeros((nblk, blk), F32)
                for m in range(qb):
                    gm = gate[m:m + 1, :]
                    ge = jnp.where(gm >= gate, 1.0, 0.0)
                    gt = jnp.where(gm > gate, 1.0, 0.0)
                    cnt = cnt + jnp.where(row_n > m, ge, jnp.where(row_n < m, gt, 0.0))
                sel = jnp.where(cnt < MOBA_TOPK, past, 0.0)
            attended = jnp.where(row_n == qb, 1.0, sel)
            bias = jnp.where(attended > 0.5, 0.0, MASK_BIAS)
            rhs_s[h, qb, h * hd:(h + 1) * hd, :] = qt[h * hd:(h + 1) * hd, :].astype(BF16)
            rhs_s[h, qb, (1 - h) * hd:(2 - h) * hd, :] = jnp.zeros((hd, blk), BF16)
            rhs_s[h, qb, LANES:LANES + bias_rows, :] = jnp.concatenate(
                [bias, jnp.zeros((bias_rows - nblk, blk), F32)], axis=0).astype(BF16)
            rhs_s[h, qb, LANES + bias_rows:, :] = jnp.zeros((LANES - bias_rows, blk), BF16)

    key_row = lax.broadcasted_iota(jnp.int32, (blk, blk), 0)
    query_col = lax.broadcasted_iota(jnp.int32, (blk, blk), 1)
    causal = key_row <= query_col

    units = [(qb, h) for qb in range(nblk) for h in range(2)]
    pending = [(u, j) for u, (qb, _) in enumerate(units) for j in range(qb + 1)]
    col_max = {}

    def emit_score():
        u, j = pending.pop(0)
        qb, h = units[u]
        s = _dot(k_s[j * blk:(j + 1) * blk, :], rhs_s[h, qb])
        if j == qb:
            s = jnp.where(causal, s, NEG_INF)
        s_s[u % SCORE_SLOTS, j] = s
        mj = jnp.max(s, axis=0, keepdims=True)
        col_max[u] = mj if u not in col_max else jnp.maximum(col_max[u], mj)

    outs = []
    for u, (qb, h) in enumerate(units):
        while pending and pending[0][0] <= u + SCORE_UNITS_AHEAD - 1:
            emit_score()
        m = col_max.pop(u)
        acc = None
        for j in range(qb + 1):
            if pending and pending[0][0] <= u + SCORE_UNITS_AHEAD:
                emit_score()
            p = jnp.exp2(s_s[u % SCORE_SLOTS, j] - m)
            pv = _dot(vt_s[j, h], p.astype(BF16))
            acc = pv if acc is None else acc + pv
        outs.append(acc[0:hd, :] / acc[hd:hd + 1, :])
        if h == 1:
            out_t = jnp.concatenate(outs, axis=0)
            o_ref[0, qb * blk:(qb + 1) * blk, :] = jnp.transpose(out_t).astype(o_ref.dtype)
            outs = []


def _attention(proj3):
    bsz, seq, _ = proj3.shape
    npairs = ATT_HEADS // 2
    nblk = seq // MOBA_BLOCK
    qkv_spec = lambda off: pl.BlockSpec((1, seq, LANES), lambda b, p: (b, 0, off // LANES + p))
    return pl.pallas_call(
        _attn_kernel,
        grid=(bsz, npairs),
        in_specs=[qkv_spec(Q_OFF), qkv_spec(K_OFF), qkv_spec(V_OFF)],
        out_specs=pl.BlockSpec((1, seq, LANES), lambda b, p: (b, 0, p)),
        out_shape=jax.ShapeDtypeStruct((bsz, seq, ATT_WIDTH), BF16),
        scratch_shapes=[pltpu.VMEM((seq, 2 * LANES), BF16),
                        pltpu.VMEM((nblk, 2, ATT_HEAD_DIM + 2 * SUBLANES, MOBA_BLOCK), BF16),
                        pltpu.VMEM((2, nblk, 2 * LANES, MOBA_BLOCK), BF16),
                        pltpu.VMEM((SCORE_SLOTS, nblk, MOBA_BLOCK, MOBA_BLOCK), F32)],
        compiler_params=_params(("parallel", "arbitrary")),
        name="moba_attention",
    )(proj3, proj3, proj3)


def _split_pack(v, lane):
    hi = v.astype(BF16).astype(F32)
    lo = v - hi
    packed = jnp.where(lane < SSM_HEADS, hi, pltpu.roll(lo, SSM_HEADS, 1))
    return jnp.where(lane < 2 * SSM_HEADS, packed, 0.0).astype(BF16)


def _ssd_kernel(xbc_ref, z_ref, dt_ref, dtb_ref, alog_ref, dskip_ref,
                normw_ref, expand_ref, tri_ref, y_ref, state_s):
    q = SSM_CHUNK
    lane = lax.broadcasted_iota(jnp.int32, (1, LANES), 1)

    @pl.when(pl.program_id(1) == 0)
    def _():
        state_s[...] = jnp.zeros(state_s.shape, F32)

    dt_in = dt_ref[...] + dtb_ref[...]
    dt = jnp.maximum(dt_in, 0.0) + jnp.log1p(jnp.exp(-jnp.abs(dt_in)))
    a_dt = dt * (-jnp.exp(alog_ref[...]))
    row = lax.broadcasted_iota(jnp.int32, (q, q), 0)
    col = lax.broadcasted_iota(jnp.int32, (q, q), 1)
    causal = col <= row
    tri = tri_ref[...]
    p1 = a_dt.astype(BF16)
    r1 = a_dt - p1.astype(F32)
    p2 = r1.astype(BF16)
    p3 = (r1 - p2.astype(F32)).astype(BF16)
    acs = (_dot(tri, p1) + _dot(tri, p2) + _dot(tri, p3)) * LOG2E
    acs_t = jnp.transpose(acs)
    exp_a = jnp.exp2(acs)
    to_end = jnp.exp2(acs[q - 1:q, :] - acs)

    expand = expand_ref[...]
    dt_x = _dot(_split_pack(dt, lane), expand)
    exp_a_x = _dot(_split_pack(exp_a, lane), expand)
    to_end_x = _dot(_split_pack(to_end, lane), expand)

    pair_lane = lane // SSM_HEAD_DIM
    for g in range(SSM_GROUPS):
        cg = slice(g * GROUP_WIDTH, (g + 1) * GROUP_WIDTH)
        xs = xbc_ref[:, cg].astype(F32)
        xdt = xs * dt_x[:, cg]
        b_off = D_INNER + g * SSM_STATE
        c_off = D_INNER + SSM_GROUPS * SSM_STATE + g * SSM_STATE
        bmat = xbc_ref[:, b_off:b_off + SSM_STATE]
        cmat = xbc_ref[:, c_off:c_off + SSM_STATE]
        cb = jnp.where(causal, _dot_nt(cmat, bmat), 0.0)
        prev = state_s[g]
        y = _dot(cmat, prev.astype(BF16)) * exp_a_x[:, cg]
        y = y + xs * dskip_ref[:, cg]
        xdt_b = xdt.astype(BF16)
        diag = []
        for pr in range(HEADS_PER_GROUP // 2):
            xp = xdt_b[:, pr * LANES:(pr + 1) * LANES]
            outs = []
            for e in range(2):
                h = g * HEADS_PER_GROUP + 2 * pr + e
                seg = acs[:, h:h + 1] - acs_t[h:h + 1, :]
                decay = jnp.exp2(jnp.where(causal, seg, 0.0))
                outs.append(_dot((cb * decay).astype(BF16), xp))
            diag.append(jnp.where(pair_lane == 0, outs[0], outs[1]))
        y = y + jnp.concatenate(diag, axis=1)

        xw = (xdt * to_end_x[:, cg]).astype(BF16)
        new_state = _dot(jnp.transpose(bmat.astype(F32)).astype(BF16), xw)
        state_s[g] = prev * exp_a_x[q - 1:q, cg] + new_state

        zg = z_ref[:, cg].astype(F32)
        hf = y * (zg * _sigmoid(zg))
        ms = jnp.mean(hf * hf, axis=1, keepdims=True)
        y_ref[:, cg] = (hf * lax.rsqrt(ms + RMS_EPS) * normw_ref[:, cg]).astype(y_ref.dtype)


def _ssd_constants():
    r = jnp.arange(LANES)[:, None]
    c = jnp.arange(D_INNER)[None, :]
    expand = ((r % SSM_HEADS) == (c // SSM_HEAD_DIM)) & (r < 2 * SSM_HEADS)
    tri = jnp.arange(SSM_CHUNK)[None, :] <= jnp.arange(SSM_CHUNK)[:, None]
    return expand.astype(BF16), tri.astype(BF16)


def _ssd(proj2, dt2, bsz, seq, dtb, alog, dskip_x, normw, constants):
    expand, tri = constants
    nc = seq // SSM_CHUNK
    q = SSM_CHUNK
    rowblk = lambda b, c: b * nc + c
    const = lambda shape: pl.BlockSpec(shape, lambda b, c: (0, 0))
    return pl.pallas_call(
        _ssd_kernel,
        grid=(bsz, nc),
        in_specs=[pl.BlockSpec((q, XBC_DIM), lambda b, c: (rowblk(b, c), XBC_OFF // XBC_DIM)),
                  pl.BlockSpec((q, D_INNER), lambda b, c: (rowblk(b, c), Z_OFF // D_INNER)),
                  pl.BlockSpec((q, LANES), lambda b, c: (rowblk(b, c), 0)),
                  const((1, LANES)), const((1, LANES)), const((1, D_INNER)), const((1, D_INNER)),
                  const((LANES, D_INNER)), const((q, q))],
        out_specs=pl.BlockSpec((q, D_INNER), lambda b, c: (rowblk(b, c), 0)),
        out_shape=jax.ShapeDtypeStruct((bsz * seq, D_INNER), BF16),
        scratch_shapes=[pltpu.VMEM((SSM_GROUPS, SSM_STATE, GROUP_WIDTH), F32)],
        compiler_params=_params(("parallel", "arbitrary")),
        name="ssd_scan",
    )(proj2, proj2, dt2, dtb, alog, dskip_x, normw, expand, tri)


def _layer_norm(r, g, b):
    mu = jnp.mean(r, axis=1, keepdims=True)
    d = r - mu
    var = jnp.mean(d * d, axis=1, keepdims=True)
    return d * lax.rsqrt(var + LN_EPS) * g + b


def _merge_kernel(att_ref, y_ref, ga_ref, gs_ref, x_ref, wa_ref, ws_ref, wo_ref, g_ref, b_ref, o_ref,
                  wa_s, ws_s, wo_s):
    @pl.when(pl.program_id(0) == 0)
    def _():
        wa_s[...] = wa_ref[...].astype(BF16)
        ws_s[...] = ws_ref[...].astype(BF16)
        wo_s[...] = wo_ref[...].astype(BF16)

    pa = _dot(att_ref[...], wa_s[...])
    ps = _dot(y_ref[...], ws_s[...])
    merged = _sigmoid(ga_ref[...].astype(F32)) * pa + _sigmoid(gs_ref[...].astype(F32)) * ps
    mix = _dot(merged.astype(BF16), wo_s[...])
    o_ref[...] = _layer_norm(ALPHA * x_ref[...] + mix, g_ref[...], b_ref[...])


def _resident(shape):
    return pl.BlockSpec(shape, lambda i: (0, 0), pipeline_mode=pl.Buffered(1))


def _merge(att2, y2, proj2, x2, wa, ws, wo, layer, g, b, tm):
    m = x2.shape[0]
    weight = lambda rows: pl.BlockSpec((None, rows, D_MODEL), lambda i: (layer, 0, 0), pipeline_mode=pl.Buffered(1))
    return pl.pallas_call(
        _merge_kernel,
        grid=(m // tm,),
        in_specs=[pl.BlockSpec((tm, ATT_WIDTH), lambda i: (i, 0)),
                  pl.BlockSpec((tm, D_INNER), lambda i: (i, 0)),
                  pl.BlockSpec((tm, D_MODEL), lambda i: (i, GA_OFF // D_MODEL)),
                  pl.BlockSpec((tm, D_MODEL), lambda i: (i, GS_OFF // D_MODEL)),
                  pl.BlockSpec((tm, D_MODEL), lambda i: (i, 0)),
                  weight(ATT_WIDTH), weight(D_INNER), weight(D_MODEL),
                  _resident((1, D_MODEL)), _resident((1, D_MODEL))],
        out_specs=pl.BlockSpec((tm, D_MODEL), lambda i: (i, 0)),
        out_shape=jax.ShapeDtypeStruct((m, D_MODEL), F32),
        scratch_shapes=[pltpu.VMEM((ATT_WIDTH, D_MODEL), BF16), pltpu.VMEM((D_INNER, D_MODEL), BF16),
                        pltpu.VMEM((D_MODEL, D_MODEL), BF16)],
        compiler_params=_params(("arbitrary",)),
        name="merge_ln",
    )(att2, y2, proj2, proj2, x2, wa, ws, wo, g, b)


def _mlp_kernel(h_ref, wu_ref, wd_ref, g_ref, b_ref, o_ref):
    h = h_ref[...]
    up = _dot(h.astype(BF16), wu_ref[...])
    act = jnp.square(jnp.maximum(up, 0.0)).astype(BF16)
    down = _dot(act, wd_ref[...])
    o_ref[...] = _layer_norm(ALPHA * h + down, g_ref[...], b_ref[...])


def _mlp(h2, wu, wd, g, b, tm):
    m = h2.shape[0]
    return pl.pallas_call(
        _mlp_kernel,
        grid=(m // tm,),
        in_specs=[pl.BlockSpec((tm, D_MODEL), lambda i: (i, 0)),
                  _resident((D_MODEL, D_FF)), _resident((D_FF, D_MODEL)),
                  _resident((1, D_MODEL)), _resident((1, D_MODEL))],
        out_specs=pl.BlockSpec((tm, D_MODEL), lambda i: (i, 0)),
        out_shape=jax.ShapeDtypeStruct((m, D_MODEL), F32),
        compiler_params=_params(("parallel",)),
        name="mlp_ln",
    )(h2, wu, wd, g, b)


RELAYOUT_CHUNK = 1024
_SRC_COLUMN = tuple(sum(IN_SPLITS[:seg]) + off
                    for seg in (4, 3, 0, 1, 2, 6, 7)
                    for off in range(0, IN_SPLITS[seg], RELAYOUT_CHUNK))
_DT_SRC = sum(IN_SPLITS[:5])


def _relayout_kernel(src_ref, o_ref):
    o_ref[...] = src_ref[0].astype(BF16)


def _source_row(c):
    tile = _SRC_COLUMN[0] // SUBLANES
    for i in range(1, len(_SRC_COLUMN)):
        tile = jnp.where(c >= i, _SRC_COLUMN[i] // SUBLANES, tile)
    return tile * SUBLANES


def _projection_weights(w_t, layer):
    assert all(r % SUBLANES == 0 for r in _SRC_COLUMN) and len(_SRC_COLUMN) * RELAYOUT_CHUNK == PROJ_WIDTH
    return pl.pallas_call(
        _relayout_kernel,
        grid=(len(_SRC_COLUMN),),
        in_specs=[pl.BlockSpec((pl.Element(1), pl.Element(RELAYOUT_CHUNK), pl.Element(D_MODEL)),
                               lambda c: (layer, _source_row(c), 0))],
        out_specs=pl.BlockSpec((RELAYOUT_CHUNK, D_MODEL), lambda c: (c, 0)),
        out_shape=jax.ShapeDtypeStruct((PROJ_WIDTH, D_MODEL), BF16),
        compiler_params=_params(("parallel",)),
        name="proj_weight_layout",
    )(w_t)


def _pad_lanes(v):
    return jnp.pad(v.astype(F32), (0, LANES - v.shape[0])).reshape(1, LANES)


def _rope_freqs():
    inv_freq = ROPE_THETA ** (-jnp.arange(0, ROT_DIM, 2, dtype=F32) / ROT_DIM)
    return inv_freq.reshape(ROT_HALF, 1)


def kernel(x, positions, ln1_g, ln1_b, w_in, conv_w, conv_b, dt_bias, a_log, d_skip, ssm_norm_w,
           w_attn_proj, w_ssm_proj, w_out, ln2_g, ln2_b, w_up, w_down):
    bsz, seq, d_model = x.shape
    assert d_model == D_MODEL and seq % MOBA_BLOCK == 0 and seq % SSM_CHUNK == 0
    assert w_in.shape == (DEPTH, D_MODEL, sum(IN_SPLITS))
    rows = bsz * seq
    tm_proj = 1024 if seq % 1024 == 0 else SSM_CHUNK
    tm_row = 512 if rows % 512 == 0 else SSM_CHUNK

    invf = _rope_freqs()
    ssd_constants = _ssd_constants()

    w_t = jnp.swapaxes(w_in, 1, 2)
    w_dt = jnp.pad(w_t[:, _DT_SRC:_DT_SRC + SSM_HEADS, :], ((0, 0), (0, LANES - SSM_HEADS), (0, 0))).astype(BF16)

    x2 = x.reshape(rows, D_MODEL)
    for i in range(DEPTH):
        w_cat = _projection_weights(w_t, i)
        proj, dt_raw = _projection(x2, positions, invf, w_cat, w_dt[i], conv_w[i], conv_b[i].reshape(1, XBC_DIM),
                                   seq, tm_proj, PROJ_WIDTH // 4)

        att = _attention(proj.reshape(bsz, seq, PROJ_WIDTH))
        y_ssm = _ssd(proj, dt_raw, bsz, seq, _pad_lanes(dt_bias[i]), _pad_lanes(a_log[i]),
                     jnp.repeat(d_skip[i].astype(F32), SSM_HEAD_DIM).reshape(1, D_INNER),
                     ssm_norm_w[i].reshape(1, D_INNER), ssd_constants)
        h = _merge(att.reshape(rows, ATT_WIDTH), y_ssm, proj, x2,
                   w_attn_proj, w_ssm_proj, w_out, i,
                   ln1_g[i].reshape(1, D_MODEL), ln1_b[i].reshape(1, D_MODEL), tm_row)
        x2 = _mlp(h, w_up[i].astype(BF16), w_down[i].astype(BF16),
                  ln2_g[i].reshape(1, D_MODEL), ln2_b[i].reshape(1, D_MODEL), tm_row)
    return x2.reshape(bsz, seq, D_MODEL)
```

```python
import functools

import jax
import jax.numpy as jnp
from jax import lax
from jax.experimental import pallas as pl
from jax.experimental.pallas import tpu as pltpu

F32 = jnp.float32
BF16 = jnp.bfloat16

D_MODEL = 1024
DEPTH = 2
ATT_HEAD_DIM = 64
ATT_HEADS = D_MODEL // ATT_HEAD_DIM
ATT_WIDTH = ATT_HEADS * ATT_HEAD_DIM
ROT_DIM = ATT_HEAD_DIM // 4
ROT_HALF = ROT_DIM // 2
ROPE_THETA = 500000.0
MOBA_BLOCK = 256
MOBA_TOPK = 3
D_INNER = 2 * D_MODEL
SSM_HEAD_DIM = 64
SSM_HEADS = D_INNER // SSM_HEAD_DIM
SSM_GROUPS = 8
SSM_STATE = 128
SSM_CONV = 4
SSM_CHUNK = 256
XBC_DIM = D_INNER + 2 * SSM_GROUPS * SSM_STATE
D_FF = 4 * D_MODEL
ALPHA = (2 * DEPTH) ** 0.25
LN_EPS = 1e-5
RMS_EPS = 1e-5
IN_SPLITS = (ATT_WIDTH, ATT_WIDTH, ATT_WIDTH, D_INNER, XBC_DIM, SSM_HEADS, D_MODEL, D_MODEL)

LANES = 128
MXU_WIDTH = 256
SUBLANES = 8
VMEM_LIMIT = 56 * 1024 * 1024

Z_OFF = 0
V_OFF = Z_OFF + D_INNER
Q_OFF = V_OFF + ATT_WIDTH
GA_OFF = Q_OFF + ATT_WIDTH
K_OFF = GA_OFF + D_MODEL
GS_OFF = K_OFF + ATT_WIDTH
REST_WIDTH = GS_OFF + D_MODEL
_REST_SEGMENTS = (3, 2, 0, 6, 1, 7)
PROJ_WIDTH = REST_WIDTH + XBC_DIM
PROJ_STEPS = 4

HEADS_PER_GROUP = SSM_HEADS // SSM_GROUPS
GROUP_WIDTH = HEADS_PER_GROUP * SSM_HEAD_DIM
NEG_INF = float("-inf")
MASK_BIAS = -1e30
LOG2E = 1.4426950408889634
Q_SCALE = ATT_HEAD_DIM ** -0.5 * LOG2E
SCORE_UNITS_AHEAD = 3
SCORE_SLOTS = SCORE_UNITS_AHEAD + 1


def _params(semantics):
    return pltpu.CompilerParams(dimension_semantics=semantics, vmem_limit_bytes=VMEM_LIMIT)


def _dot(a, b):
    return jnp.dot(a, b, preferred_element_type=F32)


def _dot_nt(a, b, precision=None):
    return lax.dot_general(a, b, (((1,), (1,)), ((), ())), precision=precision,
                           preferred_element_type=F32)


def _sigmoid(x):
    return 1.0 / (1.0 + jnp.exp2(x * (-LOG2E)))


def _proj_kernel(pos_ref, invf_ref, x_ref, wr_ref, wx_ref, wdt_ref, convw_ref, convb_ref,
                 rest_ref, xbc_ref, dt_ref, xb_ref, cos_s, sa_s, sb_s, tail_s, raw_s, *, rows_per_seq):
    i = pl.program_id(0)
    j = pl.program_id(1)
    tm = rest_ref.shape[0]

    @pl.when(j == 0)
    def _():
        xb_ref[...] = x_ref[...].astype(BF16)
        dt_ref[...] = _dot_nt(xb_ref[...], wdt_ref[...])
        ang = invf_ref[...] * pos_ref[0].astype(F32)
        c8 = jnp.cos(ang)
        s8 = jnp.sin(ang)
        one8 = jnp.ones_like(c8)
        zero8 = jnp.zeros_like(c8)
        idle = ATT_HEAD_DIM // ROT_HALF - 2
        per_head = ((cos_s, [c8, c8] + [one8] * idle),
                    (sa_s, [-s8, zero8] + [zero8] * idle),
                    (sb_s, [zero8, s8] + [zero8] * idle))
        for table, groups in per_head:
            t = jnp.transpose(jnp.concatenate(groups * (MXU_WIDTH // ATT_HEAD_DIM), axis=0))
            table[...] = t

    sub = lax.broadcasted_iota(jnp.int32, (SUBLANES, 1), 0)
    starts_sequence = (i * tm) % rows_per_seq == 0

    def conv_silu(acc, g0):
        gcols = slice(g0, g0 + MXU_WIDTH)
        tail = jnp.where(starts_sequence, 0.0, tail_s[:, gcols])
        tail_s[:, gcols] = acc[tm - SUBLANES:tm, :]
        conv = convb_ref[:, gcols] + convw_ref[SSM_CONV - 1:SSM_CONV, gcols] * acc
        for d in range(1, SSM_CONV):
            back = pltpu.roll(acc, d, 0)
            head = jnp.where(sub < d, pltpu.roll(tail, d, 0), back[0:SUBLANES, :])
            back = jnp.concatenate([head, back[SUBLANES:, :]], axis=0)
            conv = conv + convw_ref[SSM_CONV - 1 - d:SSM_CONV - d, gcols] * back
        return conv * _sigmoid(conv)

    def rope(acc):
        return (acc * cos_s[...]
                + pltpu.roll(acc, MXU_WIDTH - ROT_HALF, 1) * sa_s[...]
                + pltpu.roll(acc, ROT_HALF, 1) * sb_s[...])

    n_rest, n_xbc = rest_ref.shape[1] // MXU_WIDTH, xbc_ref.shape[1] // MXU_WIDTH

    def strip_dot(strip, slot):
        kind, c0 = strip
        w_ref = wx_ref if kind == "xbc" else wr_ref
        raw_s[slot] = _dot_nt(xb_ref[...], w_ref[c0:c0 + MXU_WIDTH, :])

    for jj in range(PROJ_STEPS):
        @pl.when(j == jj)
        def _(jj=jj):
            rest = [("rest", c * MXU_WIDTH) for c in range(n_rest)]
            xbc = [("xbc", c * MXU_WIDTH) for c in range(n_xbc)]
            order = []
            while rest or xbc:
                order += rest[:1] + xbc[:1]
                rest, xbc = rest[1:], xbc[1:]
            strip_dot(order[0], 0)
            for n, (kind, c0) in enumerate(order):
                if n + 1 < len(order):
                    strip_dot(order[n + 1], (n + 1) % 2)
                acc = raw_s[n % 2]
                cols = slice(c0, c0 + MXU_WIDTH)
                if kind == "xbc":
                    xbc_ref[:, cols] = conv_silu(acc, jj * xbc_ref.shape[1] + c0).astype(xbc_ref.dtype)
                    continue
                g0 = jj * rest_ref.shape[1] + c0
                if Q_OFF <= g0 < Q_OFF + ATT_WIDTH:
                    acc = rope(acc) * Q_SCALE
                elif K_OFF <= g0 < K_OFF + ATT_WIDTH:
                    acc = rope(acc)
                rest_ref[:, cols] = acc.astype(rest_ref.dtype)


def _projection(x, pos, invf, w, w_dt, convw, convb, rows_per_seq, tm):
    m, k = x.shape
    tr, tx = REST_WIDTH // PROJ_STEPS, XBC_DIM // PROJ_STEPS
    assert w.shape[0] == PROJ_WIDTH and rows_per_seq % tm == 0
    assert tr % MXU_WIDTH == 0 and tx % MXU_WIDTH == 0 and REST_WIDTH % tx == 0
    assert all(off % MXU_WIDTH == 0 for off in (Q_OFF, K_OFF, V_OFF, GA_OFF))
    kern = functools.partial(_proj_kernel, rows_per_seq=rows_per_seq)
    return pl.pallas_call(
        kern,
        grid=(m // tm, PROJ_STEPS),
        in_specs=[pl.BlockSpec((1, 1, tm), lambda i, j: (i, 0, 0)),
                  pl.BlockSpec((SUBLANES, 1), lambda i, j: (0, 0)),
                  pl.BlockSpec((tm, k), lambda i, j: (i, 0)),
                  pl.BlockSpec((tr, k), lambda i, j: (j, 0)),
                  pl.BlockSpec((tx, k), lambda i, j: (REST_WIDTH // tx + j, 0)),
                  pl.BlockSpec((LANES, k), lambda i, j: (0, 0)),
                  pl.BlockSpec((SSM_CONV, XBC_DIM), lambda i, j: (0, 0)),
                  pl.BlockSpec((1, XBC_DIM), lambda i, j: (0, 0))],
        out_specs=[pl.BlockSpec((tm, tr), lambda i, j: (i, j)),
                   pl.BlockSpec((tm, tx), lambda i, j: (i, j)),
                   pl.BlockSpec((tm, LANES), lambda i, j: (i, 0))],
        out_shape=[jax.ShapeDtypeStruct((m, REST_WIDTH), BF16), jax.ShapeDtypeStruct((m, XBC_DIM), BF16),
                   jax.ShapeDtypeStruct((m, LANES), F32)],
        scratch_shapes=[pltpu.VMEM((tm, k), BF16), pltpu.VMEM((tm, MXU_WIDTH), F32),
                        pltpu.VMEM((tm, MXU_WIDTH), F32), pltpu.VMEM((tm, MXU_WIDTH), F32),
                        pltpu.VMEM((SUBLANES, XBC_DIM), F32), pltpu.VMEM((2, tm, MXU_WIDTH), F32)],
        compiler_params=_params(("arbitrary", "arbitrary")),
        name="proj_matmul",
    )(pos.reshape(m // tm, 1, tm), invf, x, w, w, w_dt, convw, convb)


def _attn_kernel(q_ref, k_ref, v_ref, o_ref, k_s, vt_s, rhs_s, s_s):
    seq = q_ref.shape[1]
    nblk = seq // MOBA_BLOCK
    blk = MOBA_BLOCK
    lane = lax.broadcasted_iota(jnp.int32, (1, LANES), 1)
    head_of_lane = lane // ATT_HEAD_DIM

    @pl.when(pl.program_id(1) == 0)
    def _():
        blk_of_row = lax.broadcasted_iota(jnp.int32, (seq, LANES), 0) // blk
        k_s[:, LANES:] = jnp.where(blk_of_row == lane, 1.0, 0.0).astype(BF16)

    hd = ATT_HEAD_DIM
    extra = vt_s.shape[2] - hd
    ones_row = jnp.where(lax.broadcasted_iota(jnp.int32, (extra, blk), 0) == 0, 1.0, 0.0).astype(BF16)
    kmeans = []
    for j in range(nblk):
        rows = pl.ds(j * blk, blk)
        kj = k_ref[0, rows, :]
        k_s[rows, 0:LANES] = kj
        kmeans.append(jnp.sum(kj.astype(F32), axis=0, keepdims=True) * (1.0 / blk))
        vt = jnp.transpose(v_ref[0, rows, :].astype(F32)).astype(BF16)
        for h in range(2):
            vt_s[j, h, 0:hd, :] = vt[h * hd:(h + 1) * hd, :]
            vt_s[j, h, hd:, :] = ones_row
    kmean = jnp.concatenate(kmeans, axis=0)

    row_n = lax.broadcasted_iota(jnp.int32, (nblk, blk), 0)
    bias_rows = 2 * SUBLANES
    for qb in range(nblk):
        rows = pl.ds(qb * blk, blk)
        qt = jnp.transpose(q_ref[0, rows, :].astype(F32))
        for h in range(2):
            past = jnp.where(row_n < qb, 1.0, 0.0)
            if qb <= MOBA_TOPK:
                sel = past
            else:
                kmh = jnp.where(head_of_lane == h, kmean, 0.0)
                gate = jnp.dot(kmh, qt, precision=lax.Precision.HIGHEST,
                               preferred_element_type=F32)
                cnt = jnp.zeros((nblk, blk), F32)
                for m in range(qb):
                    gm = gate[m:m + 1, :]
                    ge = jnp.where(gm >= gate, 1.0, 0.0)
                    gt = jnp.where(gm > gate, 1.0, 0.0)
                    cnt = cnt + jnp.where(row_n > m, ge, jnp.where(row_n < m, gt, 0.0))
                sel = jnp.where(cnt < MOBA_TOPK, past, 0.0)
            attended = jnp.where(row_n == qb, 1.0, sel)
            bias = jnp.where(attended > 0.5, 0.0, MASK_BIAS)
            rhs_s[h, qb, h * hd:(h + 1) * hd, :] = qt[h * hd:(h + 1) * hd, :].astype(BF16)
            rhs_s[h, qb, (1 - h) * hd:(2 - h) * hd, :] = jnp.zeros((hd, blk), BF16)
            rhs_s[h, qb, LANES:LANES + bias_rows, :] = jnp.concatenate(
                [bias, jnp.zeros((bias_rows - nblk, blk), F32)], axis=0).astype(BF16)
            rhs_s[h, qb, LANES + bias_rows:, :] = jnp.zeros((LANES - bias_rows, blk), BF16)

    key_row = lax.broadcasted_iota(jnp.int32, (blk, blk), 0)
    query_col = lax.broadcasted_iota(jnp.int32, (blk, blk), 1)
    causal = key_row <= query_col

    units = [(qb, h) for qb in range(nblk) for h in range(2)]
    pending = [(u, j) for u, (qb, _) in enumerate(units) for j in range(qb + 1)]
    col_max = {}

    def emit_score():
        u, j = pending.pop(0)
        qb, h = units[u]
        s = _dot(k_s[j * blk:(j + 1) * blk, :], rhs_s[h, qb])
        if j == qb:
            s = jnp.where(causal, s, NEG_INF)
        s_s[u % SCORE_SLOTS, j] = s
        mj = jnp.max(s, axis=0, keepdims=True)
        col_max[u] = mj if u not in col_max else jnp.maximum(col_max[u], mj)

    outs = []
    for u, (qb, h) in enumerate(units):
        while pending and pending[0][0] <= u + SCORE_UNITS_AHEAD - 1:
            emit_score()
        m = col_max.pop(u)
        acc = None
        for j in range(qb + 1):
            if pending and pending[0][0] <= u + SCORE_UNITS_AHEAD:
                emit_score()
            p = jnp.exp2(s_s[u % SCORE_SLOTS, j] - m)
            pv = _dot(vt_s[j, h], p.astype(BF16))
            acc = pv if acc is None else acc + pv
        outs.append(acc[0:hd, :] / acc[hd:hd + 1, :])
        if h == 1:
            out_t = jnp.concatenate(outs, axis=0)
            o_ref[0, qb * blk:(qb + 1) * blk, :] = jnp.transpose(out_t).astype(o_ref.dtype)
            outs = []


def _attention(proj3):
    bsz, seq, _ = proj3.shape
    npairs = ATT_HEADS // 2
    nblk = seq // MOBA_BLOCK
    qkv_spec = lambda off: pl.BlockSpec((1, seq, LANES), lambda b, p: (b, 0, off // LANES + p))
    return pl.pallas_call(
        _attn_kernel,
        grid=(bsz, npairs),
        in_specs=[qkv_spec(Q_OFF), qkv_spec(K_OFF), qkv_spec(V_OFF)],
        out_specs=pl.BlockSpec((1, seq, LANES), lambda b, p: (b, 0, p)),
        out_shape=jax.ShapeDtypeStruct((bsz, seq, ATT_WIDTH), BF16),
        scratch_shapes=[pltpu.VMEM((seq, 2 * LANES), BF16),
                        pltpu.VMEM((nblk, 2, ATT_HEAD_DIM + 2 * SUBLANES, MOBA_BLOCK), BF16),
                        pltpu.VMEM((2, nblk, 2 * LANES, MOBA_BLOCK), BF16),
                        pltpu.VMEM((SCORE_SLOTS, nblk, MOBA_BLOCK, MOBA_BLOCK), F32)],
        compiler_params=_params(("parallel", "arbitrary")),
        name="moba_attention",
    )(proj3, proj3, proj3)


def _split_pack(v, lane):
    hi = v.astype(BF16).astype(F32)
    lo = v - hi
    packed = jnp.where(lane < SSM_HEADS, hi, pltpu.roll(lo, SSM_HEADS, 1))
    return jnp.where(lane < 2 * SSM_HEADS, packed, 0.0).astype(BF16)


def _ssd_kernel(xbc_ref, z_ref, dt_ref, dtb_ref, alog_ref, dskip_ref,
                normw_ref, expand_ref, tri_ref, y_ref, state_s):
    q = SSM_CHUNK
    lane = lax.broadcasted_iota(jnp.int32, (1, LANES), 1)

    @pl.when(pl.program_id(1) == 0)
    def _():
        state_s[...] = jnp.zeros(state_s.shape, F32)

    dt_in = dt_ref[...] + dtb_ref[...]
    dt = jnp.maximum(dt_in, 0.0) + jnp.log1p(jnp.exp(-jnp.abs(dt_in)))
    a_dt = dt * (-jnp.exp(alog_ref[...]))
    row = lax.broadcasted_iota(jnp.int32, (q, q), 0)
    col = lax.broadcasted_iota(jnp.int32, (q, q), 1)
    causal = col <= row
    tri = tri_ref[...]
    p1 = a_dt.astype(BF16)
    r1 = a_dt - p1.astype(F32)
    p2 = r1.astype(BF16)
    p3 = (r1 - p2.astype(F32)).astype(BF16)
    acs = (_dot(tri, p1) + _dot(tri, p2) + _dot(tri, p3)) * LOG2E
    acs_t = jnp.transpose(acs)
    exp_a = jnp.exp2(acs)
    to_end = jnp.exp2(acs[q - 1:q, :] - acs)

    expand = expand_ref[...]
    dt_x = _dot(_split_pack(dt, lane), expand)
    exp_a_x = _dot(_split_pack(exp_a, lane), expand)
    to_end_x = _dot(_split_pack(to_end, lane), expand)

    pair_lane = lane // SSM_HEAD_DIM
    for g in range(SSM_GROUPS):
        cg = slice(g * GROUP_WIDTH, (g + 1) * GROUP_WIDTH)
        xs = xbc_ref[:, cg].astype(F32)
        xdt = xs * dt_x[:, cg]
        b_off = D_INNER + g * SSM_STATE
        c_off = D_INNER + SSM_GROUPS * SSM_STATE + g * SSM_STATE
        bmat = xbc_ref[:, b_off:b_off + SSM_STATE]
        cmat = xbc_ref[:, c_off:c_off + SSM_STATE]
        cb = jnp.where(causal, _dot_nt(cmat, bmat), 0.0)
        prev = state_s[g]
        y = _dot(cmat, prev.astype(BF16)) * exp_a_x[:, cg]
        y = y + xs * dskip_ref[:, cg]
        xdt_b = xdt.astype(BF16)
        diag = []
        for pr in range(HEADS_PER_GROUP // 2):
            xp = xdt_b[:, pr * LANES:(pr + 1) * LANES]
            outs = []
            for e in range(2):
                h = g * HEADS_PER_GROUP + 2 * pr + e
                seg = acs[:, h:h + 1] - acs_t[h:h + 1, :]
                decay = jnp.exp2(jnp.where(causal, seg, 0.0))
                outs.append(_dot((cb * decay).astype(BF16), xp))
            diag.append(jnp.where(pair_lane == 0, outs[0], outs[1]))
        y = y + jnp.concatenate(diag, axis=1)

        xw = (xdt * to_end_x[:, cg]).astype(BF16)
        new_state = _dot(jnp.transpose(bmat.astype(F32)).astype(BF16), xw)
        state_s[g] = prev * exp_a_x[q - 1:q, cg] + new_state

        zg = z_ref[:, cg].astype(F32)
        hf = y * (zg * _sigmoid(zg))
        ms = jnp.mean(hf * hf, axis=1, keepdims=True)
        y_ref[:, cg] = (hf * lax.rsqrt(ms + RMS_EPS) * normw_ref[:, cg]).astype(y_ref.dtype)


def _ssd_constants():
    r = jnp.arange(LANES)[:, None]
    c = jnp.arange(D_INNER)[None, :]
    expand = ((r % SSM_HEADS) == (c // SSM_HEAD_DIM)) & (r < 2 * SSM_HEADS)
    tri = jnp.arange(SSM_CHUNK)[None, :] <= jnp.arange(SSM_CHUNK)[:, None]
    return expand.astype(BF16), tri.astype(BF16)


def _ssd(xbc2, rest2, dt2, bsz, seq, dtb, alog, dskip_x, normw, constants):
    expand, tri = constants
    nc = seq // SSM_CHUNK
    q = SSM_CHUNK
    rowblk = lambda b, c: b * nc + c
    const = lambda shape: pl.BlockSpec(shape, lambda b, c: (0, 0))
    return pl.pallas_call(
        _ssd_kernel,
        grid=(bsz, nc),
        in_specs=[pl.BlockSpec((q, XBC_DIM), lambda b, c: (rowblk(b, c), 0)),
                  pl.BlockSpec((q, D_INNER), lambda b, c: (rowblk(b, c), Z_OFF // D_INNER)),
                  pl.BlockSpec((q, LANES), lambda b, c: (rowblk(b, c), 0)),
                  const((1, LANES)), const((1, LANES)), const((1, D_INNER)), const((1, D_INNER)),
                  const((LANES, D_INNER)), const((q, q))],
        out_specs=pl.BlockSpec((q, D_INNER), lambda b, c: (rowblk(b, c), 0)),
        out_shape=jax.ShapeDtypeStruct((bsz * seq, D_INNER), BF16),
        scratch_shapes=[pltpu.VMEM((SSM_GROUPS, SSM_STATE, GROUP_WIDTH), F32)],
        compiler_params=_params(("parallel", "arbitrary")),
        name="ssd_scan",
    )(xbc2, rest2, dt2, dtb, alog, dskip_x, normw, expand, tri)


def _layer_norm(r, g, b):
    mu = jnp.mean(r, axis=1, keepdims=True)
    d = r - mu
    var = jnp.mean(d * d, axis=1, keepdims=True)
    return d * lax.rsqrt(var + LN_EPS) * g + b


def _merge_kernel(att_ref, y_ref, ga_ref, gs_ref, x_ref, wa_ref, ws_ref, wo_ref, g_ref, b_ref, o_ref,
                  wa_s, ws_s, wo_s):
    @pl.when(pl.program_id(0) == 0)
    def _():
        wa_s[...] = wa_ref[...].astype(BF16)
        ws_s[...] = ws_ref[...].astype(BF16)
        wo_s[...] = wo_ref[...].astype(BF16)

    tm = o_ref.shape[0]
    halves = (slice(0, tm // 2), slice(tm // 2, tm))
    branches = [(_dot(att_ref[r, :], wa_s[...]), _dot(y_ref[r, :], ws_s[...])) for r in halves]
    mixes = []
    for r, (pa, ps) in zip(halves, branches):
        merged = _sigmoid(ga_ref[r, :].astype(F32)) * pa + _sigmoid(gs_ref[r, :].astype(F32)) * ps
        mixes.append(_dot(merged.astype(BF16), wo_s[...]))
    for r, mix in zip(halves, mixes):
        o_ref[r, :] = _layer_norm(ALPHA * x_ref[r, :] + mix, g_ref[...], b_ref[...])


def _resident(shape):
    return pl.BlockSpec(shape, lambda i: (0, 0), pipeline_mode=pl.Buffered(1))


def _merge(att2, y2, proj2, x2, wa, ws, wo, layer, g, b, tm):
    m = x2.shape[0]
    weight = lambda rows: pl.BlockSpec((None, rows, D_MODEL), lambda i: (layer, 0, 0), pipeline_mode=pl.Buffered(1))
    return pl.pallas_call(
        _merge_kernel,
        grid=(m // tm,),
        in_specs=[pl.BlockSpec((tm, ATT_WIDTH), lambda i: (i, 0)),
                  pl.BlockSpec((tm, D_INNER), lambda i: (i, 0)),
                  pl.BlockSpec((tm, D_MODEL), lambda i: (i, GA_OFF // D_MODEL)),
                  pl.BlockSpec((tm, D_MODEL), lambda i: (i, GS_OFF // D_MODEL)),
                  pl.BlockSpec((tm, D_MODEL), lambda i: (i, 0)),
                  weight(ATT_WIDTH), weight(D_INNER), weight(D_MODEL),
                  _resident((1, D_MODEL)), _resident((1, D_MODEL))],
        out_specs=pl.BlockSpec((tm, D_MODEL), lambda i: (i, 0)),
        out_shape=jax.ShapeDtypeStruct((m, D_MODEL), F32),
        scratch_shapes=[pltpu.VMEM((ATT_WIDTH, D_MODEL), BF16), pltpu.VMEM((D_INNER, D_MODEL), BF16),
                        pltpu.VMEM((D_MODEL, D_MODEL), BF16)],
        compiler_params=_params(("arbitrary",)),
        name="merge_ln",
    )(att2, y2, proj2, proj2, x2, wa, ws, wo, g, b)


def _mlp_kernel(h_ref, wu_ref, wd_ref, g_ref, b_ref, o_ref):
    h = h_ref[...]
    up = _dot(h.astype(BF16), wu_ref[...])
    act = jnp.square(jnp.maximum(up, 0.0)).astype(BF16)
    down = _dot(act, wd_ref[...])
    o_ref[...] = _layer_norm(ALPHA * h + down, g_ref[...], b_ref[...])


def _mlp(h2, wu, wd, g, b, tm):
    m = h2.shape[0]
    return pl.pallas_call(
        _mlp_kernel,
        grid=(m // tm,),
        in_specs=[pl.BlockSpec((tm, D_MODEL), lambda i: (i, 0)),
                  _resident((D_MODEL, D_FF)), _resident((D_FF, D_MODEL)),
                  _resident((1, D_MODEL)), _resident((1, D_MODEL))],
        out_specs=pl.BlockSpec((tm, D_MODEL), lambda i: (i, 0)),
        out_shape=jax.ShapeDtypeStruct((m, D_MODEL), F32),
        compiler_params=_params(("parallel",)),
        name="mlp_ln",
    )(h2, wu, wd, g, b)


RELAYOUT_CHUNK = 1024
_SRC_COLUMN = tuple(sum(IN_SPLITS[:seg]) + off
                    for seg in _REST_SEGMENTS + (4,)
                    for off in range(0, IN_SPLITS[seg], RELAYOUT_CHUNK))
_DT_SRC = sum(IN_SPLITS[:5])


def _relayout_kernel(src_ref, o_ref):
    o_ref[...] = src_ref[0].astype(BF16)


def _source_row(c):
    tile = _SRC_COLUMN[0] // SUBLANES
    for i in range(1, len(_SRC_COLUMN)):
        tile = jnp.where(c >= i, _SRC_COLUMN[i] // SUBLANES, tile)
    return tile * SUBLANES


def _projection_weights(w_t, layer):
    assert all(r % SUBLANES == 0 for r in _SRC_COLUMN) and len(_SRC_COLUMN) * RELAYOUT_CHUNK == PROJ_WIDTH
    return pl.pallas_call(
        _relayout_kernel,
        grid=(len(_SRC_COLUMN),),
        in_specs=[pl.BlockSpec((pl.Element(1), pl.Element(RELAYOUT_CHUNK), pl.Element(D_MODEL)),
                               lambda c: (layer, _source_row(c), 0))],
        out_specs=pl.BlockSpec((RELAYOUT_CHUNK, D_MODEL), lambda c: (c, 0)),
        out_shape=jax.ShapeDtypeStruct((PROJ_WIDTH, D_MODEL), BF16),
        compiler_params=_params(("parallel",)),
        name="proj_weight_layout",
    )(w_t)


def _pad_lanes(v):
    return jnp.pad(v.astype(F32), (0, LANES - v.shape[0])).reshape(1, LANES)


def _rope_freqs():
    inv_freq = ROPE_THETA ** (-jnp.arange(0, ROT_DIM, 2, dtype=F32) / ROT_DIM)
    return inv_freq.reshape(ROT_HALF, 1)


def kernel(x, positions, ln1_g, ln1_b, w_in, conv_w, conv_b, dt_bias, a_log, d_skip, ssm_norm_w,
           w_attn_proj, w_ssm_proj, w_out, ln2_g, ln2_b, w_up, w_down):
    bsz, seq, d_model = x.shape
    assert d_model == D_MODEL and seq % MOBA_BLOCK == 0 and seq % SSM_CHUNK == 0
    assert w_in.shape == (DEPTH, D_MODEL, sum(IN_SPLITS))
    rows = bsz * seq
    tm_proj = 1024 if seq % 1024 == 0 else SSM_CHUNK
    tm_row = 512 if rows % 512 == 0 else SSM_CHUNK

    invf = _rope_freqs()
    ssd_constants = _ssd_constants()

    w_t = jnp.swapaxes(w_in, 1, 2)
    w_dt = jnp.pad(w_t[:, _DT_SRC:_DT_SRC + SSM_HEADS, :], ((0, 0), (0, LANES - SSM_HEADS), (0, 0))).astype(BF16)

    x2 = x.reshape(rows, D_MODEL)
    for i in range(DEPTH):
        w_cat = _projection_weights(w_t, i)
        proj, xbc, dt_raw = _projection(x2, positions, invf, w_cat, w_dt[i], conv_w[i],
                                        conv_b[i].reshape(1, XBC_DIM), seq, tm_proj)

        att = _attention(proj.reshape(bsz, seq, REST_WIDTH))
        y_ssm = _ssd(xbc, proj, dt_raw, bsz, seq, _pad_lanes(dt_bias[i]), _pad_lanes(a_log[i]),
                     jnp.repeat(d_skip[i].astype(F32), SSM_HEAD_DIM).reshape(1, D_INNER),
                     ssm_norm_w[i].reshape(1, D_INNER), ssd_constants)
        h = _merge(att.reshape(rows, ATT_WIDTH), y_ssm, proj, x2,
                   w_attn_proj, w_ssm_proj, w_out, i,
                   ln1_g[i].reshape(1, D_MODEL), ln1_b[i].reshape(1, D_MODEL), tm_row)
        x2 = _mlp(h, w_up[i].astype(BF16), w_down[i].astype(BF16),
                  ln2_g[i].reshape(1, D_MODEL), ln2_b[i].reshape(1, D_MODEL), tm_row)
    return x2.reshape(bsz, seq, D_MODEL)
```

```python
import functools

import jax
import jax.numpy as jnp
from jax import lax
from jax.experimental import pallas as pl
from jax.experimental.pallas import tpu as pltpu

F32 = jnp.float32
BF16 = jnp.bfloat16

D_MODEL = 1024
DEPTH = 2
ATT_HEAD_DIM = 64
ATT_HEADS = D_MODEL // ATT_HEAD_DIM
ATT_WIDTH = ATT_HEADS * ATT_HEAD_DIM
ROT_DIM = ATT_HEAD_DIM // 4
ROT_HALF = ROT_DIM // 2
ROPE_THETA = 500000.0
MOBA_BLOCK = 256
MOBA_TOPK = 3
D_INNER = 2 * D_MODEL
SSM_HEAD_DIM = 64
SSM_HEADS = D_INNER // SSM_HEAD_DIM
SSM_GROUPS = 8
SSM_STATE = 128
SSM_CONV = 4
SSM_CHUNK = 256
XBC_DIM = D_INNER + 2 * SSM_GROUPS * SSM_STATE
D_FF = 4 * D_MODEL
ALPHA = (2 * DEPTH) ** 0.25
LN_EPS = 1e-5
RMS_EPS = 1e-5
IN_SPLITS = (ATT_WIDTH, ATT_WIDTH, ATT_WIDTH, D_INNER, XBC_DIM, SSM_HEADS, D_MODEL, D_MODEL)

LANES = 128
MXU_WIDTH = 256
SUBLANES = 8
VMEM_LIMIT = 56 * 1024 * 1024

Z_OFF = 0
V_OFF = Z_OFF + D_INNER
Q_OFF = V_OFF + ATT_WIDTH
GA_OFF = Q_OFF + ATT_WIDTH
K_OFF = GA_OFF + D_MODEL
GS_OFF = K_OFF + ATT_WIDTH
REST_WIDTH = GS_OFF + D_MODEL
_REST_SEGMENTS = (3, 2, 0, 6, 1, 7)
PROJ_WIDTH = REST_WIDTH + XBC_DIM
PROJ_STEPS = 4

HEADS_PER_GROUP = SSM_HEADS // SSM_GROUPS
GROUP_WIDTH = HEADS_PER_GROUP * SSM_HEAD_DIM
NEG_INF = float("-inf")
MASK_BIAS = -1e30
LOG2E = 1.4426950408889634
Q_SCALE = ATT_HEAD_DIM ** -0.5 * LOG2E
SCORE_UNITS_AHEAD = 3
SCORE_SLOTS = SCORE_UNITS_AHEAD + 1


def _params(semantics):
    return pltpu.CompilerParams(dimension_semantics=semantics, vmem_limit_bytes=VMEM_LIMIT)


def _dot(a, b):
    return jnp.dot(a, b, preferred_element_type=F32)


def _dot_nt(a, b, precision=None):
    return lax.dot_general(a, b, (((1,), (1,)), ((), ())), precision=precision,
                           preferred_element_type=F32)


def _sigmoid(x):
    return 1.0 / (1.0 + jnp.exp2(x * (-LOG2E)))


def _proj_kernel(pos_ref, invf_ref, x_ref, wr_ref, wx_ref, wdt_ref, convw_ref, convb_ref,
                 rest_ref, xbc_ref, dt_ref, xb_ref, cos_s, sa_s, sb_s, tail_s, raw_s, *, rows_per_seq):
    i = pl.program_id(0)
    j = pl.program_id(1)
    tm = rest_ref.shape[0]

    @pl.when(j == 0)
    def _():
        xb_ref[...] = x_ref[...].astype(BF16)
        dt_ref[...] = _dot_nt(xb_ref[...], wdt_ref[...])
        ang = invf_ref[...] * pos_ref[0].astype(F32)
        c8 = jnp.cos(ang)
        s8 = jnp.sin(ang)
        one8 = jnp.ones_like(c8)
        zero8 = jnp.zeros_like(c8)
        idle = ATT_HEAD_DIM // ROT_HALF - 2
        per_head = ((cos_s, [c8, c8] + [one8] * idle),
                    (sa_s, [-s8, zero8] + [zero8] * idle),
                    (sb_s, [zero8, s8] + [zero8] * idle))
        for table, groups in per_head:
            t = jnp.transpose(jnp.concatenate(groups * (MXU_WIDTH // ATT_HEAD_DIM), axis=0))
            table[...] = t

    sub = lax.broadcasted_iota(jnp.int32, (SUBLANES, 1), 0)
    starts_sequence = (i * tm) % rows_per_seq == 0

    def conv_silu(acc, g0):
        gcols = slice(g0, g0 + MXU_WIDTH)
        tail = jnp.where(starts_sequence, 0.0, tail_s[:, gcols])
        tail_s[:, gcols] = acc[tm - SUBLANES:tm, :]
        conv = convb_ref[:, gcols] + convw_ref[SSM_CONV - 1:SSM_CONV, gcols] * acc
        for d in range(1, SSM_CONV):
            back = pltpu.roll(acc, d, 0)
            head = jnp.where(sub < d, pltpu.roll(tail, d, 0), back[0:SUBLANES, :])
            back = jnp.concatenate([head, back[SUBLANES:, :]], axis=0)
            conv = conv + convw_ref[SSM_CONV - 1 - d:SSM_CONV - d, gcols] * back
        return conv * _sigmoid(conv)

    def rope(acc):
        return (acc * cos_s[...]
                + pltpu.roll(acc, MXU_WIDTH - ROT_HALF, 1) * sa_s[...]
                + pltpu.roll(acc, ROT_HALF, 1) * sb_s[...])

    n_rest, n_xbc = rest_ref.shape[1] // MXU_WIDTH, xbc_ref.shape[1] // MXU_WIDTH

    def strip_dot(strip, slot):
        kind, c0 = strip
        w_ref = wx_ref if kind == "xbc" else wr_ref
        raw_s[slot] = _dot_nt(xb_ref[...], w_ref[c0:c0 + MXU_WIDTH, :])

    for jj in range(PROJ_STEPS):
        @pl.when(j == jj)
        def _(jj=jj):
            rest = [("rest", c * MXU_WIDTH) for c in range(n_rest)]
            xbc = [("xbc", c * MXU_WIDTH) for c in range(n_xbc)]
            order = []
            while rest or xbc:
                order += rest[:1] + xbc[:1]
                rest, xbc = rest[1:], xbc[1:]
            strip_dot(order[0], 0)
            for n, (kind, c0) in enumerate(order):
                if n + 1 < len(order):
                    strip_dot(order[n + 1], (n + 1) % 2)
                acc = raw_s[n % 2]
                cols = slice(c0, c0 + MXU_WIDTH)
                if kind == "xbc":
                    xbc_ref[:, cols] = conv_silu(acc, jj * xbc_ref.shape[1] + c0).astype(xbc_ref.dtype)
                    continue
                g0 = jj * rest_ref.shape[1] + c0
                if Q_OFF <= g0 < Q_OFF + ATT_WIDTH:
                    acc = rope(acc) * Q_SCALE
                elif K_OFF <= g0 < K_OFF + ATT_WIDTH:
                    acc = rope(acc)
                elif Z_OFF <= g0 < Z_OFF + D_INNER:
                    acc = acc * _sigmoid(acc)
                elif GA_OFF <= g0 < GA_OFF + D_MODEL or GS_OFF <= g0 < GS_OFF + D_MODEL:
                    acc = _sigmoid(acc)
                rest_ref[:, cols] = acc.astype(rest_ref.dtype)


def _projection(x, pos, invf, w, w_dt, convw, convb, rows_per_seq, tm):
    m, k = x.shape
    tr, tx = REST_WIDTH // PROJ_STEPS, XBC_DIM // PROJ_STEPS
    assert w.shape[0] == PROJ_WIDTH and rows_per_seq % tm == 0
    assert tr % MXU_WIDTH == 0 and tx % MXU_WIDTH == 0 and REST_WIDTH % tx == 0
    assert all(off % MXU_WIDTH == 0 for off in (Q_OFF, K_OFF, V_OFF, GA_OFF))
    kern = functools.partial(_proj_kernel, rows_per_seq=rows_per_seq)
    return pl.pallas_call(
        kern,
        grid=(m // tm, PROJ_STEPS),
        in_specs=[pl.BlockSpec((1, 1, tm), lambda i, j: (i, 0, 0)),
                  pl.BlockSpec((SUBLANES, 1), lambda i, j: (0, 0)),
                  pl.BlockSpec((tm, k), lambda i, j: (i, 0)),
                  pl.BlockSpec((tr, k), lambda i, j: (j, 0)),
                  pl.BlockSpec((tx, k), lambda i, j: (REST_WIDTH // tx + j, 0)),
                  pl.BlockSpec((LANES, k), lambda i, j: (0, 0)),
                  pl.BlockSpec((SSM_CONV, XBC_DIM), lambda i, j: (0, 0)),
                  pl.BlockSpec((1, XBC_DIM), lambda i, j: (0, 0))],
        out_specs=[pl.BlockSpec((tm, tr), lambda i, j: (i, j)),
                   pl.BlockSpec((tm, tx), lambda i, j: (i, j)),
                   pl.BlockSpec((tm, LANES), lambda i, j: (i, 0))],
        out_shape=[jax.ShapeDtypeStruct((m, REST_WIDTH), BF16), jax.ShapeDtypeStruct((m, XBC_DIM), BF16),
                   jax.ShapeDtypeStruct((m, LANES), F32)],
        scratch_shapes=[pltpu.VMEM((tm, k), BF16), pltpu.VMEM((tm, MXU_WIDTH), F32),
                        pltpu.VMEM((tm, MXU_WIDTH), F32), pltpu.VMEM((tm, MXU_WIDTH), F32),
                        pltpu.VMEM((SUBLANES, XBC_DIM), F32), pltpu.VMEM((2, tm, MXU_WIDTH), F32)],
        compiler_params=_params(("arbitrary", "arbitrary")),
        name="proj_matmul",
    )(pos.reshape(m // tm, 1, tm), invf, x, w, w, w_dt, convw, convb)


def _attn_kernel(q_ref, k_ref, v_ref, o_ref, k_s, vt_s, rhs_s, s_s):
    seq = q_ref.shape[1]
    nblk = seq // MOBA_BLOCK
    blk = MOBA_BLOCK
    lane = lax.broadcasted_iota(jnp.int32, (1, LANES), 1)
    head_of_lane = lane // ATT_HEAD_DIM

    @pl.when(pl.program_id(1) == 0)
    def _():
        blk_of_row = lax.broadcasted_iota(jnp.int32, (seq, LANES), 0) // blk
        k_s[:, LANES:] = jnp.where(blk_of_row == lane, 1.0, 0.0).astype(BF16)

    hd = ATT_HEAD_DIM
    extra = vt_s.shape[2] - hd
    ones_row = jnp.where(lax.broadcasted_iota(jnp.int32, (extra, blk), 0) == 0, 1.0, 0.0).astype(BF16)
    kmeans = []
    for j in range(nblk):
        rows = pl.ds(j * blk, blk)
        kj = k_ref[0, rows, :]
        k_s[rows, 0:LANES] = kj
        kmeans.append(jnp.sum(kj.astype(F32), axis=0, keepdims=True) * (1.0 / blk))
        vt = jnp.transpose(v_ref[0, rows, :].astype(F32)).astype(BF16)
        for h in range(2):
            vt_s[j, h, 0:hd, :] = vt[h * hd:(h + 1) * hd, :]
            vt_s[j, h, hd:, :] = ones_row
    kmean = jnp.concatenate(kmeans, axis=0)

    row_n = lax.broadcasted_iota(jnp.int32, (nblk, blk), 0)
    bias_rows = 2 * SUBLANES
    for qb in range(nblk):
        rows = pl.ds(qb * blk, blk)
        qt = jnp.transpose(q_ref[0, rows, :].astype(F32))
        for h in range(2):
            past = jnp.where(row_n < qb, 1.0, 0.0)
            if qb <= MOBA_TOPK:
                sel = past
            else:
                kmh = jnp.where(head_of_lane == h, kmean, 0.0)
                gate = jnp.dot(kmh, qt, precision=lax.Precision.HIGHEST,
                               preferred_element_type=F32)
                cnt = jnp.zeros((nblk, blk), F32)
                for m in range(qb):
                    gm = gate[m:m + 1, :]
                    ge = jnp.where(gm >= gate, 1.0, 0.0)
                    gt = jnp.where(gm > gate, 1.0, 0.0)
                    cnt = cnt + jnp.where(row_n > m, ge, jnp.where(row_n < m, gt, 0.0))
                sel = jnp.where(cnt < MOBA_TOPK, past, 0.0)
            attended = jnp.where(row_n == qb, 1.0, sel)
            bias = jnp.where(attended > 0.5, 0.0, MASK_BIAS)
            rhs_s[h, qb, h * hd:(h + 1) * hd, :] = qt[h * hd:(h + 1) * hd, :].astype(BF16)
            rhs_s[h, qb, (1 - h) * hd:(2 - h) * hd, :] = jnp.zeros((hd, blk), BF16)
            rhs_s[h, qb, LANES:LANES + bias_rows, :] = jnp.concatenate(
                [bias, jnp.zeros((bias_rows - nblk, blk), F32)], axis=0).astype(BF16)
            rhs_s[h, qb, LANES + bias_rows:, :] = jnp.zeros((LANES - bias_rows, blk), BF16)

    key_row = lax.broadcasted_iota(jnp.int32, (blk, blk), 0)
    query_col = lax.broadcasted_iota(jnp.int32, (blk, blk), 1)
    causal = key_row <= query_col

    units = [(qb, h) for qb in range(nblk) for h in range(2)]
    pending = [(u, j) for u, (qb, _) in enumerate(units) for j in range(qb + 1)]
    col_max = {}

    def emit_score():
        u, j = pending.pop(0)
        qb, h = units[u]
        s = _dot(k_s[j * blk:(j + 1) * blk, :], rhs_s[h, qb])
        if j == qb:
            s = jnp.where(causal, s, NEG_INF)
        s_s[u % SCORE_SLOTS, j] = s
        mj = jnp.max(s, axis=0, keepdims=True)
        col_max[u] = mj if u not in col_max else jnp.maximum(col_max[u], mj)

    outs = []
    for u, (qb, h) in enumerate(units):
        while pending and pending[0][0] <= u + SCORE_UNITS_AHEAD - 1:
            emit_score()
        m = col_max.pop(u)
        acc = None
        for j in range(qb + 1):
            if pending and pending[0][0] <= u + SCORE_UNITS_AHEAD:
                emit_score()
            p = jnp.exp2(s_s[u % SCORE_SLOTS, j] - m)
            pv = _dot(vt_s[j, h], p.astype(BF16))
            acc = pv if acc is None else acc + pv
        outs.append(acc[0:hd, :] / acc[hd:hd + 1, :])
        if h == 1:
            out_t = jnp.concatenate(outs, axis=0)
            o_ref[0, qb * blk:(qb + 1) * blk, :] = jnp.transpose(out_t).astype(o_ref.dtype)
            outs = []


def _attention(proj3):
    bsz, seq, _ = proj3.shape
    npairs = ATT_HEADS // 2
    nblk = seq // MOBA_BLOCK
    qkv_spec = lambda off: pl.BlockSpec((1, seq, LANES), lambda b, p: (b, 0, off // LANES + p))
    return pl.pallas_call(
        _attn_kernel,
        grid=(bsz, npairs),
        in_specs=[qkv_spec(Q_OFF), qkv_spec(K_OFF), qkv_spec(V_OFF)],
        out_specs=pl.BlockSpec((1, seq, LANES), lambda b, p: (b, 0, p)),
        out_shape=jax.ShapeDtypeStruct((bsz, seq, ATT_WIDTH), BF16),
        scratch_shapes=[pltpu.VMEM((seq, 2 * LANES), BF16),
                        pltpu.VMEM((nblk, 2, ATT_HEAD_DIM + 2 * SUBLANES, MOBA_BLOCK), BF16),
                        pltpu.VMEM((2, nblk, 2 * LANES, MOBA_BLOCK), BF16),
                        pltpu.VMEM((SCORE_SLOTS, nblk, MOBA_BLOCK, MOBA_BLOCK), F32)],
        compiler_params=_params(("parallel", "arbitrary")),
        name="moba_attention",
    )(proj3, proj3, proj3)


def _split_pack(v, lane):
    hi = v.astype(BF16).astype(F32)
    lo = v - hi
    packed = jnp.where(lane < SSM_HEADS, hi, pltpu.roll(lo, SSM_HEADS, 1))
    return jnp.where(lane < 2 * SSM_HEADS, packed, 0.0).astype(BF16)


def _ssd_kernel(xbc_ref, z_ref, dt_ref, dtb_ref, alog_ref, dskip_ref,
                normw_ref, expand_ref, tri_ref, y_ref, state_s):
    q = SSM_CHUNK
    lane = lax.broadcasted_iota(jnp.int32, (1, LANES), 1)

    @pl.when(pl.program_id(1) == 0)
    def _():
        state_s[...] = jnp.zeros(state_s.shape, F32)

    dt_in = dt_ref[...] + dtb_ref[...]
    dt = jnp.maximum(dt_in, 0.0) + jnp.log1p(jnp.exp(-jnp.abs(dt_in)))
    a_dt = dt * (-jnp.exp(alog_ref[...]))
    row = lax.broadcasted_iota(jnp.int32, (q, q), 0)
    col = lax.broadcasted_iota(jnp.int32, (q, q), 1)
    causal = col <= row
    tri = tri_ref[...]
    p1 = a_dt.astype(BF16)
    r1 = a_dt - p1.astype(F32)
    p2 = r1.astype(BF16)
    p3 = (r1 - p2.astype(F32)).astype(BF16)
    acs = (_dot(tri, p1) + _dot(tri, p2) + _dot(tri, p3)) * LOG2E
    acs_t = jnp.transpose(acs)
    exp_a = jnp.exp2(acs)
    to_end = jnp.exp2(acs[q - 1:q, :] - acs)

    expand = expand_ref[...]
    dt_x = _dot(_split_pack(dt, lane), expand)
    exp_a_x = _dot(_split_pack(exp_a, lane), expand)
    to_end_x = _dot(_split_pack(to_end, lane), expand)

    pair_lane = lane // SSM_HEAD_DIM
    for g in range(SSM_GROUPS):
        cg = slice(g * GROUP_WIDTH, (g + 1) * GROUP_WIDTH)
        xs = xbc_ref[:, cg].astype(F32)
        xdt = xs * dt_x[:, cg]
        b_off = D_INNER + g * SSM_STATE
        c_off = D_INNER + SSM_GROUPS * SSM_STATE + g * SSM_STATE
        bmat = xbc_ref[:, b_off:b_off + SSM_STATE]
        cmat = xbc_ref[:, c_off:c_off + SSM_STATE]
        cb = jnp.where(causal, _dot_nt(cmat, bmat), 0.0)
        prev = state_s[g]
        y = _dot(cmat, prev.astype(BF16)) * exp_a_x[:, cg]
        y = y + xs * dskip_ref[:, cg]
        xdt_b = xdt.astype(BF16)
        half = q // 2
        diag = []
        for pr in range(HEADS_PER_GROUP // 2):
            xp = xdt_b[:, pr * LANES:(pr + 1) * LANES]
            outs = []
            for e in range(2):
                h = g * HEADS_PER_GROUP + 2 * pr + e
                col, row_v = acs[:, h:h + 1], acs_t[h:h + 1, :]
                top = jnp.exp2(jnp.where(causal[:half, :half], col[:half] - row_v[:, :half], 0.0))
                bottom = jnp.exp2(jnp.where(causal[half:], col[half:] - row_v, 0.0))
                outs.append(jnp.concatenate(
                    [_dot((cb[:half, :half] * top).astype(BF16), xp[:half]),
                     _dot((cb[half:] * bottom).astype(BF16), xp)], axis=0))
            diag.append(jnp.where(pair_lane == 0, outs[0], outs[1]))
        y = y + jnp.concatenate(diag, axis=1)

        xw = (xdt * to_end_x[:, cg]).astype(BF16)
        new_state = _dot(jnp.transpose(bmat.astype(F32)).astype(BF16), xw)
        state_s[g] = prev * exp_a_x[q - 1:q, cg] + new_state

        hf = y * z_ref[:, cg].astype(F32)
        ms = jnp.mean(hf * hf, axis=1, keepdims=True)
        y_ref[:, cg] = (hf * lax.rsqrt(ms + RMS_EPS) * normw_ref[:, cg]).astype(y_ref.dtype)


def _ssd_constants():
    r = jnp.arange(LANES)[:, None]
    c = jnp.arange(D_INNER)[None, :]
    expand = ((r % SSM_HEADS) == (c // SSM_HEAD_DIM)) & (r < 2 * SSM_HEADS)
    tri = jnp.arange(SSM_CHUNK)[None, :] <= jnp.arange(SSM_CHUNK)[:, None]
    return expand.astype(BF16), tri.astype(BF16)


def _ssd(xbc2, rest2, dt2, bsz, seq, dtb, alog, dskip_x, normw, constants):
    expand, tri = constants
    nc = seq // SSM_CHUNK
    q = SSM_CHUNK
    rowblk = lambda b, c: b * nc + c
    const = lambda shape: pl.BlockSpec(shape, lambda b, c: (0, 0))
    return pl.pallas_call(
        _ssd_kernel,
        grid=(bsz, nc),
        in_specs=[pl.BlockSpec((q, XBC_DIM), lambda b, c: (rowblk(b, c), 0)),
                  pl.BlockSpec((q, D_INNER), lambda b, c: (rowblk(b, c), Z_OFF // D_INNER)),
                  pl.BlockSpec((q, LANES), lambda b, c: (rowblk(b, c), 0)),
                  const((1, LANES)), const((1, LANES)), const((1, D_INNER)), const((1, D_INNER)),
                  const((LANES, D_INNER)), const((q, q))],
        out_specs=pl.BlockSpec((q, D_INNER), lambda b, c: (rowblk(b, c), 0)),
        out_shape=jax.ShapeDtypeStruct((bsz * seq, D_INNER), BF16),
        scratch_shapes=[pltpu.VMEM((SSM_GROUPS, SSM_STATE, GROUP_WIDTH), F32)],
        compiler_params=_params(("parallel", "arbitrary")),
        name="ssd_scan",
    )(xbc2, rest2, dt2, dtb, alog, dskip_x, normw, expand, tri)


def _layer_norm(r, g, b):
    mu = jnp.mean(r, axis=1, keepdims=True)
    d = r - mu
    var = jnp.mean(d * d, axis=1, keepdims=True)
    return d * lax.rsqrt(var + LN_EPS) * g + b


def _merge_kernel(att_ref, y_ref, ga_ref, gs_ref, x_ref, wa_ref, ws_ref, wo_ref, g_ref, b_ref, o_ref,
                  wa_s, ws_s, wo_s):
    @pl.when(pl.program_id(0) == 0)
    def _():
        wa_s[...] = wa_ref[...].astype(BF16)
        ws_s[...] = ws_ref[...].astype(BF16)
        wo_s[...] = wo_ref[...].astype(BF16)

    tm = o_ref.shape[0]
    halves = (slice(0, tm // 2), slice(tm // 2, tm))
    branches = [(_dot(att_ref[r, :], wa_s[...]), _dot(y_ref[r, :], ws_s[...])) for r in halves]
    mixes = []
    for r, (pa, ps) in zip(halves, branches):
        merged = ga_ref[r, :].astype(F32) * pa + gs_ref[r, :].astype(F32) * ps
        mixes.append(_dot(merged.astype(BF16), wo_s[...]))
    for r, mix in zip(halves, mixes):
        o_ref[r, :] = _layer_norm(ALPHA * x_ref[r, :] + mix, g_ref[...], b_ref[...])


def _resident(shape):
    return pl.BlockSpec(shape, lambda i: (0, 0), pipeline_mode=pl.Buffered(1))


def _merge(att2, y2, proj2, x2, wa, ws, wo, layer, g, b, tm):
    m = x2.shape[0]
    weight = lambda rows: pl.BlockSpec((None, rows, D_MODEL), lambda i: (layer, 0, 0), pipeline_mode=pl.Buffered(1))
    return pl.pallas_call(
        _merge_kernel,
        grid=(m // tm,),
        in_specs=[pl.BlockSpec((tm, ATT_WIDTH), lambda i: (i, 0)),
                  pl.BlockSpec((tm, D_INNER), lambda i: (i, 0)),
                  pl.BlockSpec((tm, D_MODEL), lambda i: (i, GA_OFF // D_MODEL)),
                  pl.BlockSpec((tm, D_MODEL), lambda i: (i, GS_OFF // D_MODEL)),
                  pl.BlockSpec((tm, D_MODEL), lambda i: (i, 0)),
                  weight(ATT_WIDTH), weight(D_INNER), weight(D_MODEL),
                  _resident((1, D_MODEL)), _resident((1, D_MODEL))],
        out_specs=pl.BlockSpec((tm, D_MODEL), lambda i: (i, 0)),
        out_shape=jax.ShapeDtypeStruct((m, D_MODEL), F32),
        scratch_shapes=[pltpu.VMEM((ATT_WIDTH, D_MODEL), BF16), pltpu.VMEM((D_INNER, D_MODEL), BF16),
                        pltpu.VMEM((D_MODEL, D_MODEL), BF16)],
        compiler_params=_params(("arbitrary",)),
        name="merge_ln",
    )(att2, y2, proj2, proj2, x2, wa, ws, wo, g, b)


def _mlp_kernel(h_ref, wu_ref, wd_ref, g_ref, b_ref, o_ref):
    h = h_ref[...]
    up = _dot(h.astype(BF16), wu_ref[...])
    act = jnp.square(jnp.maximum(up, 0.0)).astype(BF16)
    down = _dot(act, wd_ref[...])
    o_ref[...] = _layer_norm(ALPHA * h + down, g_ref[...], b_ref[...])


def _mlp(h2, wu, wd, g, b, tm):
    m = h2.shape[0]
    return pl.pallas_call(
        _mlp_kernel,
        grid=(m // tm,),
        in_specs=[pl.BlockSpec((tm, D_MODEL), lambda i: (i, 0)),
                  _resident((D_MODEL, D_FF)), _resident((D_FF, D_MODEL)),
                  _resident((1, D_MODEL)), _resident((1, D_MODEL))],
        out_specs=pl.BlockSpec((tm, D_MODEL), lambda i: (i, 0)),
        out_shape=jax.ShapeDtypeStruct((m, D_MODEL), F32),
        compiler_params=_params(("parallel",)),
        name="mlp_ln",
    )(h2, wu, wd, g, b)


RELAYOUT_CHUNK = 1024
_SRC_COLUMN = tuple(sum(IN_SPLITS[:seg]) + off
                    for seg in _REST_SEGMENTS + (4,)
                    for off in range(0, IN_SPLITS[seg], RELAYOUT_CHUNK))
_DT_SRC = sum(IN_SPLITS[:5])


def _relayout_kernel(src_ref, o_ref):
    o_ref[...] = src_ref[0].astype(BF16)


def _source_row(c):
    tile = _SRC_COLUMN[0] // SUBLANES
    for i in range(1, len(_SRC_COLUMN)):
        tile = jnp.where(c >= i, _SRC_COLUMN[i] // SUBLANES, tile)
    return tile * SUBLANES


def _projection_weights(w_t, layer):
    assert all(r % SUBLANES == 0 for r in _SRC_COLUMN) and len(_SRC_COLUMN) * RELAYOUT_CHUNK == PROJ_WIDTH
    return pl.pallas_call(
        _relayout_kernel,
        grid=(len(_SRC_COLUMN),),
        in_specs=[pl.BlockSpec((pl.Element(1), pl.Element(RELAYOUT_CHUNK), pl.Element(D_MODEL)),
                               lambda c: (layer, _source_row(c), 0))],
        out_specs=pl.BlockSpec((RELAYOUT_CHUNK, D_MODEL), lambda c: (c, 0)),
        out_shape=jax.ShapeDtypeStruct((PROJ_WIDTH, D_MODEL), BF16),
        compiler_params=_params(("parallel",)),
        name="proj_weight_layout",
    )(w_t)


def _pad_lanes(v):
    return jnp.pad(v.astype(F32), (0, LANES - v.shape[0])).reshape(1, LANES)


def _rope_freqs():
    inv_freq = ROPE_THETA ** (-jnp.arange(0, ROT_DIM, 2, dtype=F32) / ROT_DIM)
    return inv_freq.reshape(ROT_HALF, 1)


def kernel(x, positions, ln1_g, ln1_b, w_in, conv_w, conv_b, dt_bias, a_log, d_skip, ssm_norm_w,
           w_attn_proj, w_ssm_proj, w_out, ln2_g, ln2_b, w_up, w_down):
    bsz, seq, d_model = x.shape
    assert d_model == D_MODEL and seq % MOBA_BLOCK == 0 and seq % SSM_CHUNK == 0
    assert w_in.shape == (DEPTH, D_MODEL, sum(IN_SPLITS))
    rows = bsz * seq
    tm_proj = 1024 if seq % 1024 == 0 else SSM_CHUNK
    tm_row = 512 if rows % 512 == 0 else SSM_CHUNK

    invf = _rope_freqs()
    ssd_constants = _ssd_constants()

    w_t = jnp.swapaxes(w_in, 1, 2)
    w_dt = jnp.pad(w_t[:, _DT_SRC:_DT_SRC + SSM_HEADS, :], ((0, 0), (0, LANES - SSM_HEADS), (0, 0))).astype(BF16)

    x2 = x.reshape(rows, D_MODEL)
    for i in range(DEPTH):
        w_cat = _projection_weights(w_t, i)
        proj, xbc, dt_raw = _projection(x2, positions, invf, w_cat, w_dt[i], conv_w[i],
                                        conv_b[i].reshape(1, XBC_DIM), seq, tm_proj)

        att = _attention(proj.reshape(bsz, seq, REST_WIDTH))
        y_ssm = _ssd(xbc, proj, dt_raw, bsz, seq, _pad_lanes(dt_bias[i]), _pad_lanes(a_log[i]),
                     jnp.repeat(d_skip[i].astype(F32), SSM_HEAD_DIM).reshape(1, D_INNER),
                     ssm_norm_w[i].reshape(1, D_INNER), ssd_constants)
        h = _merge(att.reshape(rows, ATT_WIDTH), y_ssm, proj, x2,
                   w_attn_proj, w_ssm_proj, w_out, i,
                   ln1_g[i].reshape(1, D_MODEL), ln1_b[i].reshape(1, D_MODEL), tm_row)
        x2 = _mlp(h, w_up[i].astype(BF16), w_down[i].astype(BF16),
                  ln2_g[i].reshape(1, D_MODEL), ln2_b[i].reshape(1, D_MODEL), tm_row)
    return x2.reshape(bsz, seq, D_MODEL)
```
